```python
import jax, jax.numpy as jnp
from jax import lax
import numpy as np

D_MODEL = 1024
BATCH = 4
SEQ = 4096
DEPTH = 4
DEC_BATCH = 32
DEC_SEQ = 1
PAST_LEN = 8192
PAGE_SIZE = 128

N_MIXERS = 2
N_ATTN_LAYERS = (DEPTH + N_MIXERS - 1) // N_MIXERS
N_CONV_LAYERS = DEPTH // N_MIXERS
N_HEADS = 8
HEAD_DIM = D_MODEL // N_HEADS
ATTN_DIM = N_HEADS * HEAD_DIM
IDX_HEADS = 8
IDX_DIM = 128
TOPK_MAX = 256
ATTN_Q_BLOCK = 128
IN_COLS = 3 * ATTN_DIM + IDX_HEADS * IDX_DIM + IDX_DIM + IDX_HEADS
CONV_WIDTH = 31
CONV_CTX = CONV_WIDTH - 1
D_FF = ((8 * D_MODEL // 3 + 127) // 128) * 128
PLE_DIM = 256
EPS = 1e-6

kernel_name = 'dsa_conformer_macaron_hybrid_step'


def _rms_norm(x, g):
    xf = x.astype(jnp.float32)
    y = xf * lax.rsqrt(jnp.mean(xf * xf, axis=-1, keepdims=True) + EPS)
    return (y * g.astype(jnp.float32)).astype(x.dtype)


def _layer_norm(x, g, b):
    xf = x.astype(jnp.float32)
    xc = xf - jnp.mean(xf, axis=-1, keepdims=True)
    y = xc * lax.rsqrt(jnp.mean(xc * xc, axis=-1, keepdims=True) + EPS)
    return (y * g.astype(jnp.float32) + b.astype(jnp.float32)).astype(x.dtype)


def _swiglu_ffn(x, g, w_gate, w_up, w_down):
    h = _rms_norm(x, g)
    return (jax.nn.silu(h @ w_gate) * (h @ w_up)) @ w_down


def _per_layer_embed(x, p, g, w_gate, w_proj):
    return jax.nn.sigmoid(_rms_norm(x, g) @ w_gate) * (p @ w_proj)


def _attn_project(h, w_in, q_gain, k_gain, kidx_gain):
    B, T, _ = h.shape
    z = h @ w_in
    o1 = ATTN_DIM
    o2 = 2 * ATTN_DIM
    o3 = 3 * ATTN_DIM
    o4 = o3 + IDX_HEADS * IDX_DIM
    o5 = o4 + IDX_DIM
    q = _rms_norm(z[..., :o1].reshape(B, T, N_HEADS, HEAD_DIM), q_gain)
    k = _rms_norm(z[..., o1:o2].reshape(B, T, N_HEADS, HEAD_DIM), k_gain)
    v = z[..., o2:o3].reshape(B, T, N_HEADS, HEAD_DIM)
    qi = z[..., o3:o4].reshape(B, T, IDX_HEADS, IDX_DIM)
    ki = _rms_norm(z[..., o4:o5], kidx_gain)
    wi = z[..., o5:]
    return q, k, v, qi, ki, wi


def _index_select(qi, wi, ki_keys, q_pos, topk):
    dots = jnp.einsum('bthd,bsd->bths', qi, ki_keys) * (IDX_DIM ** -0.5)
    score = jnp.einsum('bths,bth->bts', jax.nn.relu(dots), wi * (IDX_HEADS ** -0.5))
    key_pos = jnp.arange(ki_keys.shape[1])
    allowed = key_pos[None, :] <= q_pos[:, None]
    score = jnp.where(allowed[None], score.astype(jnp.float32), -jnp.inf)
    _, idx = lax.top_k(score, topk)
    valid = idx <= q_pos[None, :, None]
    return idx, valid


def _attend_gathered(q, kg, vg, valid):
    s = jnp.einsum('bthd,btkhd->bthk', q, kg).astype(jnp.float32) * (HEAD_DIM ** -0.5)
    s = jnp.where(valid[:, :, None, :], s, -jnp.inf)
    p = jax.nn.softmax(s, axis=-1).astype(vg.dtype)
    return jnp.einsum('bthk,btkhd->bthd', p, vg)


def _dsa_prompt(q, k, v, qi, ki, wi, topk):
    B, S = q.shape[0], q.shape[1]
    nb = S // ATTN_Q_BLOCK
    bidx = jnp.arange(B)[:, None, None]

    def to_blocks(a):
        return jnp.swapaxes(a.reshape((B, nb, ATTN_Q_BLOCK) + a.shape[2:]), 0, 1)

    def block(args):
        qb, qib, wib, start = args
        q_pos = start + jnp.arange(ATTN_Q_BLOCK)
        idx, valid = _index_select(qib, wib, ki, q_pos, topk)
        return _attend_gathered(qb, k[bidx, idx], v[bidx, idx], valid)

    starts = jnp.arange(nb) * ATTN_Q_BLOCK
    out = lax.map(block, (to_blocks(q), to_blocks(qi), to_blocks(wi), starts))
    return jnp.swapaxes(out, 0, 1).reshape(q.shape)


def _dsa_sample(q, k, v, qi, ki, wi, cache_k, cache_v, cache_kidx, page_table, a, topk):
    DB, T = q.shape[0], q.shape[1]
    past = page_table.shape[1] * PAGE_SIZE
    ki_past = cache_kidx[a, page_table].reshape(DB, past, IDX_DIM)
    ki_all = jnp.concatenate([ki_past, ki.astype(ki_past.dtype)], axis=1)
    q_pos = past + jnp.arange(T)
    idx, valid = _index_select(qi, wi, ki_all, q_pos, topk)
    bidx = jnp.arange(DB)[:, None, None]
    idx_c = jnp.minimum(idx, past - 1)
    phys = page_table[bidx, idx_c // PAGE_SIZE]
    off = idx_c % PAGE_SIZE
    idx_n = jnp.clip(idx - past, 0, T - 1)
    in_past = (idx < past)[..., None, None]
    kg = jnp.where(in_past, cache_k[a, phys, off], k[bidx, idx_n].astype(cache_k.dtype))
    vg = jnp.where(in_past, cache_v[a, phys, off], v[bidx, idx_n].astype(cache_v.dtype))
    return _attend_gathered(q, kg, vg, valid)


def _conv_module(h, ctx, w_in, b_in, w_dw, b_dw, ln_g, ln_b, w_out, b_out):
    u = h @ w_in + b_in
    u = u[..., :D_MODEL] * jax.nn.sigmoid(u[..., D_MODEL:])
    full = jnp.concatenate([ctx.astype(u.dtype), u], axis=1)
    y = lax.conv_general_dilated(full, w_dw[:, None, :].astype(full.dtype), window_strides=(1,),
                                 padding='VALID', dimension_numbers=('NWC', 'WIO', 'NWC'),
                                 feature_group_count=D_MODEL) + b_dw
    y = jax.nn.silu(_layer_norm(y, ln_g, ln_b))
    return y @ w_out + b_out, full[:, -CONV_CTX:]


def setup_inputs(seed: int = 0) -> dict:
    key = jax.random.key(seed)
    keys = jax.random.split(key, 40)
    counter = [0]
    f32 = jnp.float32

    def nk():
        counter[0] += 1
        return keys[counter[0] - 1]

    def nrm(shape, scale=1.0):
        return jax.random.normal(nk(), shape, f32) * scale

    def gain(shape):
        return 1.0 + nrm(shape, 0.05)

    n_pages = PAST_LEN // PAGE_SIZE
    n_used = DEC_BATCH * n_pages
    n_phys = n_used + (n_used + 3) // 4
    page_table = jax.random.permutation(nk(), n_phys)[:n_used].reshape(DEC_BATCH, n_pages).astype(jnp.int32)
    return {
        'x_prompt': nrm((BATCH, SEQ, D_MODEL)),
        'x_sample': nrm((DEC_BATCH, DEC_SEQ, D_MODEL)),
        'p_prompt': nrm((DEPTH, BATCH, SEQ, PLE_DIM)),
        'p_sample': nrm((DEPTH, DEC_BATCH, DEC_SEQ, PLE_DIM)),
        'cache_k': nrm((N_ATTN_LAYERS, n_phys, PAGE_SIZE, N_HEADS, HEAD_DIM)),
        'cache_v': nrm((N_ATTN_LAYERS, n_phys, PAGE_SIZE, N_HEADS, HEAD_DIM)),
        'cache_kidx': nrm((N_ATTN_LAYERS, n_phys, PAGE_SIZE, IDX_DIM)),
        'state_conv': nrm((N_CONV_LAYERS, DEC_BATCH, CONV_CTX, D_MODEL), 0.5),
        'page_table': page_table,
        'ffn1_norm': gain((DEPTH, D_MODEL)),
        'ffn1_w_gate': nrm((DEPTH, D_MODEL, D_FF), D_MODEL ** -0.5),
        'ffn1_w_up': nrm((DEPTH, D_MODEL, D_FF), D_MODEL ** -0.5),
        'ffn1_w_down': nrm((DEPTH, D_FF, D_MODEL), D_FF ** -0.5),
        'mix_norm': gain((DEPTH, D_MODEL)),
        'attn_w_in': nrm((N_ATTN_LAYERS, D_MODEL, IN_COLS), D_MODEL ** -0.5),
        'attn_q_gain': gain((N_ATTN_LAYERS, HEAD_DIM)),
        'attn_k_gain': gain((N_ATTN_LAYERS, HEAD_DIM)),
        'attn_kidx_gain': gain((N_ATTN_LAYERS, IDX_DIM)),
        'attn_w_o': nrm((N_ATTN_LAYERS, ATTN_DIM, D_MODEL), ATTN_DIM ** -0.5),
        'conv_w_in': nrm((N_CONV_LAYERS, D_MODEL, 2 * D_MODEL), D_MODEL ** -0.5),
        'conv_b_in': nrm((N_CONV_LAYERS, 2 * D_MODEL), 0.01),
        'conv_w_dw': nrm((N_CONV_LAYERS, CONV_WIDTH, D_MODEL), CONV_WIDTH ** -0.5),
        'conv_b_dw': nrm((N_CONV_LAYERS, D_MODEL), 0.01),
        'conv_ln_g': gain((N_CONV_LAYERS, D_MODEL)),
        'conv_ln_b': nrm((N_CONV_LAYERS, D_MODEL), 0.01),
        'conv_w_out': nrm((N_CONV_LAYERS, D_MODEL, D_MODEL), D_MODEL ** -0.5),
        'conv_b_out': nrm((N_CONV_LAYERS, D_MODEL), 0.01),
        'ffn2_norm': gain((DEPTH, D_MODEL)),
        'ffn2_w_gate': nrm((DEPTH, D_MODEL, D_FF), D_MODEL ** -0.5),
        'ffn2_w_up': nrm((DEPTH, D_MODEL, D_FF), D_MODEL ** -0.5),
        'ffn2_w_down': nrm((DEPTH, D_FF, D_MODEL), D_FF ** -0.5),
        'ple_norm': gain((DEPTH, D_MODEL)),
        'ple_w_gate': nrm((DEPTH, D_MODEL, D_MODEL), D_MODEL ** -0.5),
        'ple_w_proj': nrm((DEPTH, PLE_DIM, D_MODEL), PLE_DIM ** -0.5),
    }


def reference(x_prompt, x_sample, p_prompt, p_sample, cache_k, cache_v, cache_kidx, state_conv,
              page_table, ffn1_norm, ffn1_w_gate, ffn1_w_up, ffn1_w_down, mix_norm,
              attn_w_in, attn_q_gain, attn_k_gain, attn_kidx_gain, attn_w_o,
              conv_w_in, conv_b_in, conv_w_dw, conv_b_dw, conv_ln_g, conv_ln_b,
              conv_w_out, conv_b_out, ffn2_norm, ffn2_w_gate, ffn2_w_up, ffn2_w_down,
              ple_norm, ple_w_gate, ple_w_proj):
    B, S, _ = x_prompt.shape
    DB, T, _ = x_sample.shape
    past = page_table.shape[1] * PAGE_SIZE
    topk_p = min(TOPK_MAX, S // 4)
    topk_s = min(TOPK_MAX, (past + T) // 4)
    hp, hs = x_prompt, x_sample
    ctx0 = jnp.zeros((B, CONV_CTX, D_MODEL), x_prompt.dtype)
    k_p, v_p, ki_p, conv_p = [], [], [], []
    k_s, v_s, ki_s, conv_s = [], [], [], []
    for i in range(DEPTH):
        ffn1 = (ffn1_norm[i], ffn1_w_gate[i], ffn1_w_up[i], ffn1_w_down[i])
        hp = hp + 0.5 * _swiglu_ffn(hp, *ffn1)
        hs = hs + 0.5 * _swiglu_ffn(hs, *ffn1)
        hp_n = _rms_norm(hp, mix_norm[i])
        hs_n = _rms_norm(hs, mix_norm[i])
        if i % N_MIXERS == 0:
            a = i // N_MIXERS
            proj = (attn_w_in[a], attn_q_gain[a], attn_k_gain[a], attn_kidx_gain[a])
            q, k, v, qi, ki, wi = _attn_project(hp_n, *proj)
            o = _dsa_prompt(q, k, v, qi, ki, wi, topk_p)
            hp = hp + o.reshape(B, S, ATTN_DIM) @ attn_w_o[a]
            k_p.append(k)
            v_p.append(v)
            ki_p.append(ki)
            q, k, v, qi, ki, wi = _attn_project(hs_n, *proj)
            o = _dsa_sample(q, k, v, qi, ki, wi, cache_k, cache_v, cache_kidx, page_table, a, topk_s)
            hs = hs + o.reshape(DB, T, ATTN_DIM) @ attn_w_o[a]
            k_s.append(k)
            v_s.append(v)
            ki_s.append(ki)
        else:
            c = i // N_MIXERS
            conv = (conv_w_in[c], conv_b_in[c], conv_w_dw[c], conv_b_dw[c], conv_ln_g[c],
                    conv_ln_b[c], conv_w_out[c], conv_b_out[c])
            o, st = _conv_module(hp_n, ctx0, *conv)
            hp = hp + o
            conv_p.append(st)
            o, st = _conv_module(hs_n, state_conv[c], *conv)
            hs = hs + o
            conv_s.append(st)
        ffn2 = (ffn2_norm[i], ffn2_w_gate[i], ffn2_w_up[i], ffn2_w_down[i])
        hp = hp + 0.5 * _swiglu_ffn(hp, *ffn2)
        hs = hs + 0.5 * _swiglu_ffn(hs, *ffn2)
        hp = hp + _per_layer_embed(hp, p_prompt[i], ple_norm[i], ple_w_gate[i], ple_w_proj[i])
        hs = hs + _per_layer_embed(hs, p_sample[i], ple_norm[i], ple_w_gate[i], ple_w_proj[i])
    return (hp, hs, jnp.stack(k_p), jnp.stack(v_p), jnp.stack(ki_p), jnp.stack(conv_p),
            jnp.stack(k_s), jnp.stack(v_s), jnp.stack(ki_s), jnp.stack(conv_s))
```

```python
import functools

import jax
import jax.numpy as jnp
from jax import lax
from jax.experimental import pallas as pl
from jax.experimental.pallas import tpu as pltpu

F32 = jnp.float32
BF16 = jnp.bfloat16

N_HEADS = 8
HEAD_DIM = 128
IDX_HEADS = 8
IDX_DIM = 128
TOPK_MAX = 256
PAGE_SIZE = 128
CONV_WIDTH = 31
CONV_CTX = CONV_WIDTH - 1
EPS = 1e-6

LANES = 128
VMEM_LIMIT_BYTES = 56 * 1024 * 1024
ROW_TILE = 512
FF_CHUNK = 256
ATTN_Q_TILE = 128
ATTN_K_CHUNK = 512
CONV_T_TILE = 128
CONV_HALO = 32
CONV_ROW_BLOCK = 32
N_BISECT = 30
IDX_PAGES_PER_STEP = 8
KV_PAGES_PER_STEP = 4

_NT = (((1,), (1,)), ((), ()))


def _compiler_params(semantics):
    return pltpu.CompilerParams(dimension_semantics=semantics, vmem_limit_bytes=VMEM_LIMIT_BYTES)


def _resident(shape):
    nd = len(shape)
    return pl.BlockSpec(shape, lambda *_: (0,) * nd, pipeline_mode=pl.Buffered(1))


def _rows(tile, cols):
    return pl.BlockSpec((tile, cols), lambda i: (i, 0))


def _rms(x, g):
    return x * lax.rsqrt(jnp.mean(x * x, axis=-1, keepdims=True) + EPS) * g


def _dot(a, b):
    return jnp.dot(a, b, preferred_element_type=F32)


def _ffn_half_step(x, g_ref, wg_ref, wu_ref, wd_ref, act_ref):
    hn = _rms(x, g_ref[...]).astype(BF16)
    d_ff = wg_ref.shape[1]
    for c in range(d_ff // FF_CHUNK):
        sl = slice(c * FF_CHUNK, (c + 1) * FF_CHUNK)
        g = _dot(hn, wg_ref[:, sl])
        u = _dot(hn, wu_ref[:, sl])
        act_ref[:, sl] = (g * jax.nn.sigmoid(g) * u).astype(BF16)
    return x + 0.5 * _dot(act_ref[...], wd_ref[...])


def _stage_a_attn_kernel(x_ref, g1_ref, wg_ref, wu_ref, wd_ref, gm_ref, wq_ref, wk_ref, wv_ref, wqi_ref,
                         wki_ref, wwi_ref, qg_ref, kg_ref, kig_ref,
                         h_ref, qb_ref, kf_ref, vf_ref, kif_ref, kb_ref, vb_ref, qib_ref, kib_ref, wi_ref,
                         act_ref):
    h = _ffn_half_step(x_ref[...], g1_ref, wg_ref, wu_ref, wd_ref, act_ref)
    h_ref[...] = h
    hn = _rms(h, gm_ref[...]).astype(BF16)

    zq = _dot(hn, wq_ref[...])
    for hh in range(N_HEADS):
        sl = slice(hh * HEAD_DIM, (hh + 1) * HEAD_DIM)
        qb_ref[:, sl] = (_rms(zq[:, sl], qg_ref[...]) * (HEAD_DIM ** -0.5)).astype(BF16)
    zk = _dot(hn, wk_ref[...])
    for hh in range(N_HEADS):
        sl = slice(hh * HEAD_DIM, (hh + 1) * HEAD_DIM)
        kn = _rms(zk[:, sl], kg_ref[...])
        kf_ref[:, sl] = kn
        kb_ref[:, sl] = kn.astype(BF16)
    zv = _dot(hn, wv_ref[...])
    vf_ref[...] = zv
    vb_ref[...] = zv.astype(BF16)
    qib_ref[...] = _dot(hn, wqi_ref[...]).astype(BF16)
    kin = _rms(_dot(hn, wki_ref[...]), kig_ref[...])
    kif_ref[...] = kin
    kib_ref[...] = kin.astype(BF16)
    wi_ref[...] = _dot(hn, wwi_ref[...])


def _stage_a_attn(x, ffn, g_mix, proj, tile):
    n, d = x.shape
    g1, wg, wu, wd = ffn
    wq, wk, wv, wqi, wki, wwi, qg, kg, kig = proj
    d_ff = wg.shape[1]
    consts = (g1, wg, wu, wd, g_mix, wq, wk, wv, wqi, wki, wwi, qg, kg, kig)
    sds = jax.ShapeDtypeStruct
    out_shape = (sds((n, d), F32), sds((n, d), BF16), sds((n, d), F32), sds((n, d), F32),
                 sds((n, IDX_DIM), F32), sds((n, d), BF16), sds((n, d), BF16), sds((n, d), BF16),
                 sds((n, IDX_DIM), BF16), sds((n, LANES), F32))
    return pl.pallas_call(
        _stage_a_attn_kernel,
        grid=(n // tile,),
        in_specs=[_rows(tile, d)] + [_resident(c.shape) for c in consts],
        out_specs=[_rows(tile, s.shape[1]) for s in out_shape],
        out_shape=out_shape,
        scratch_shapes=[pltpu.VMEM((tile, d_ff), BF16)],
        compiler_params=_compiler_params(("parallel",)),
        name="stage_a_attn",
    )(x, *consts)


def _stage_a_conv_kernel(x_ref, g1_ref, wg_ref, wu_ref, wd_ref, gm_ref, wci_ref, bci_ref,
                         h_ref, glu_ref, act_ref):
    h = _ffn_half_step(x_ref[...], g1_ref, wg_ref, wu_ref, wd_ref, act_ref)
    h_ref[...] = h
    hn = _rms(h, gm_ref[...]).astype(BF16)
    d = h.shape[1]
    u = _dot(hn, wci_ref[...]) + bci_ref[...]
    glu_ref[...] = u[:, :d] * jax.nn.sigmoid(u[:, d:])


def _stage_a_conv(x, ffn, g_mix, wci, bci, tile):
    n, d = x.shape
    g1, wg, wu, wd = ffn
    consts = (g1, wg, wu, wd, g_mix, wci, bci)
    out_shape = (jax.ShapeDtypeStruct((n, d), F32), jax.ShapeDtypeStruct((n, d), F32))
    return pl.pallas_call(
        _stage_a_conv_kernel,
        grid=(n // tile,),
        in_specs=[_rows(tile, d)] + [_resident(c.shape) for c in consts],
        out_specs=[_rows(tile, d), _rows(tile, d)],
        out_shape=out_shape,
        scratch_shapes=[pltpu.VMEM((tile, wg.shape[1]), BF16)],
        compiler_params=_compiler_params(("parallel",)),
        name="stage_a_conv",
    )(x, *consts)


def _stage_c_kernel(h_ref, m_ref, p_ref, wo_ref, bo_ref, g2_ref, wg_ref, wu_ref, wd_ref, gp_ref, wpg_ref,
                    wpp_ref, out_ref, act_ref):
    h = h_ref[...] + _dot(m_ref[...], wo_ref[...]) + bo_ref[...]
    h = _ffn_half_step(h, g2_ref, wg_ref, wu_ref, wd_ref, act_ref)
    gate = jax.nn.sigmoid(_dot(_rms(h, gp_ref[...]).astype(BF16), wpg_ref[...]))
    out_ref[...] = h + gate * _dot(p_ref[...].astype(BF16), wpp_ref[...])


def _stage_c(h, m, p, wo, bo, ffn, ple, tile):
    n, d = h.shape
    g2, wg, wu, wd = ffn
    gp, wpg, wpp = ple
    consts = (wo, bo, g2, wg, wu, wd, gp, wpg, wpp)
    return pl.pallas_call(
        _stage_c_kernel,
        grid=(n // tile,),
        in_specs=[_rows(tile, d), _rows(tile, d), _rows(tile, p.shape[1])] + [_resident(c.shape) for c in consts],
        out_specs=_rows(tile, d),
        out_shape=jax.ShapeDtypeStruct((n, d), F32),
        scratch_shapes=[pltpu.VMEM((tile, wg.shape[1]), BF16)],
        compiler_params=_compiler_params(("parallel",)),
        name="stage_c",
    )(h, m, p, *consts)


def _any(flag):
    return jnp.max(jnp.where(flag, 1.0, 0.0)).astype(jnp.int32)


def _topk_bracket(count_ge, max_below, row_min, row_max, n_valid, kk):
    def update(lo, hi, clo, chi, p, c):
        pend = clo != kk
        up = pend & (c >= kk)
        dn = pend & (c < kk)
        return (jnp.where(up, p, lo), jnp.where(dn, p, hi), jnp.where(up, c, clo), jnp.where(dn, c, chi))

    state = update(row_min, jnp.full_like(row_min, jnp.inf), n_valid, jnp.zeros_like(row_min),
                   row_max, count_ge(row_max))

    def bisect_cond(carry):
        return (carry[0] < N_BISECT) & (carry[1] > 0)

    def bisect_body(carry):
        it, _, lo, hi, clo, chi = carry
        p = 0.5 * lo + 0.5 * hi
        st = update(lo, hi, clo, chi, p, count_ge(p))
        return (it + 1, _any(st[2] != kk)) + st

    carry = lax.while_loop(bisect_cond, bisect_body, (jnp.int32(0), _any(state[2] != kk)) + state)
    lo, hi, clo, chi = carry[2:]
    tied = jnp.zeros_like(lo)

    def peel_cond(carry):
        return carry[0] > 0

    def peel_body(carry):
        _, lo, hi, clo, chi, tied = carry
        m = max_below(hi)
        c = count_ge(m)
        pend = (clo != kk) & (tied == 0.0)
        fin = pend & (c >= kk)
        dn = pend & (c < kk)
        lo = jnp.where(fin, m, lo)
        clo = jnp.where(fin, c, clo)
        tied = jnp.where(fin, 1.0, tied)
        hi = jnp.where(dn, m, hi)
        chi = jnp.where(dn, c, chi)
        return (_any((clo != kk) & (tied == 0.0)), lo, hi, clo, chi, tied)

    carry = lax.while_loop(peel_cond, peel_body, (_any(clo != kk), lo, hi, clo, chi, tied))
    return carry[1:5]


def _fold_lanes(x, op):
    out = x[:, :LANES]
    for t in range(1, x.shape[1] // LANES):
        out = op(out, x[:, t * LANES:(t + 1) * LANES])
    return out


def _attn_prompt_kernel(q_ref, qi_ref, wi_ref, k_ref, v_ref, ki_ref, o_ref, score_ref, s_ref, wb_ref, *, topk):
    tq = q_ref.shape[0]
    seq = k_ref.shape[0]
    kc = min(ATTN_K_CHUNK, seq)
    q0 = pl.program_id(1) * tq
    n_ch = (q0 + tq + kc - 1) // kc
    q_pos = q0 + lax.broadcasted_iota(jnp.int32, (tq, 1), 0)

    def chunk(c):
        return pl.ds(pl.multiple_of(c * kc, kc), kc)

    wi = wi_ref[...]
    for hh in range(IDX_HEADS):
        wb_ref[hh] = jnp.broadcast_to(wi[:, hh:hh + 1], (tq, kc))

    def score_body(c, carry):
        mx, mn = carry
        kic = ki_ref[chunk(c), :]
        acc = jnp.zeros((tq, kc), F32)
        for hh in range(IDX_HEADS):
            d = lax.dot_general(qi_ref[:, hh * IDX_DIM:(hh + 1) * IDX_DIM], kic, _NT, preferred_element_type=F32)
            acc = acc + wb_ref[hh] * jnp.maximum(d, 0.0)
        key_pos = c * kc + lax.broadcasted_iota(jnp.int32, (tq, kc), 1)
        allowed = key_pos <= q_pos
        score_ref[:, chunk(c)] = jnp.where(allowed, acc, -jnp.inf)
        mx = jnp.maximum(mx, _fold_lanes(jnp.where(allowed, acc, -jnp.inf), jnp.maximum))
        mn = jnp.minimum(mn, _fold_lanes(jnp.where(allowed, acc, jnp.inf), jnp.minimum))
        return mx, mn

    mx, mn = lax.fori_loop(0, n_ch, score_body,
                           (jnp.full((tq, LANES), -jnp.inf, F32), jnp.full((tq, LANES), jnp.inf, F32)))
    row_max = jnp.max(mx, axis=1, keepdims=True)
    row_min = jnp.min(mn, axis=1, keepdims=True)

    def count_ge(p):
        def body(c, acc):
            x = score_ref[:, chunk(c)]
            return acc + _fold_lanes(jnp.where(x >= p, 1.0, 0.0), jnp.add)
        acc = lax.fori_loop(0, n_ch, body, jnp.zeros((tq, LANES), F32))
        return jnp.sum(acc, axis=1, keepdims=True)

    def max_below(hi):
        def body(c, acc):
            x = score_ref[:, chunk(c)]
            return jnp.maximum(acc, _fold_lanes(jnp.where(x < hi, x, -jnp.inf), jnp.maximum))
        acc = lax.fori_loop(0, n_ch, body, jnp.full((tq, LANES), -jnp.inf, F32))
        return jnp.max(acc, axis=1, keepdims=True)

    n_valid = (q_pos + 1).astype(F32)
    kk = jnp.minimum(n_valid, float(topk))
    lo, hi, clo, chi = _topk_bracket(count_ge, max_below, row_min, row_max, n_valid, kk)
    has_ties = _any(clo != kk) > 0

    @pl.when(jnp.logical_not(has_ties))
    def _():
        def body(c, carry):
            x = score_ref[:, chunk(c)]
            score_ref[:, chunk(c)] = jnp.where(x >= lo, 0.0, -jnp.inf)
            return carry
        lax.fori_loop(0, n_ch, body, 0)

    @pl.when(has_ties)
    def _():
        need = kk - chi
        tri = jnp.where(lax.broadcasted_iota(jnp.int32, (kc, kc), 0) <= lax.broadcasted_iota(jnp.int32, (kc, kc), 1),
                        1.0, 0.0).astype(BF16)

        def body(c, before):
            x = score_ref[:, chunk(c)]
            cand = jnp.where((x >= lo) & (x < hi), 1.0, 0.0)
            rank = before + _dot(cand.astype(BF16), tri)
            take = (cand > 0.0) & (rank <= need)
            score_ref[:, chunk(c)] = jnp.where((x >= hi) | take, 0.0, -jnp.inf)
            return before + jnp.sum(cand, axis=1, keepdims=True)
        lax.fori_loop(0, n_ch, body, jnp.zeros((tq, 1), F32))

    for hh in range(N_HEADS):
        hs = slice(hh * HEAD_DIM, (hh + 1) * HEAD_DIM)

        def logits_body(c, mx):
            s = lax.dot_general(q_ref[:, hs], k_ref[chunk(c), hs], _NT, preferred_element_type=F32)
            s = s + score_ref[:, chunk(c)]
            s_ref[:, chunk(c)] = s
            return jnp.maximum(mx, _fold_lanes(s, jnp.maximum))

        mx = lax.fori_loop(0, n_ch, logits_body, jnp.full((tq, LANES), -jnp.inf, F32))
        m = jnp.max(mx, axis=1, keepdims=True)

        def pv_body(c, carry):
            l, acc = carry
            p = jnp.exp(s_ref[:, chunk(c)] - m)
            l = l + _fold_lanes(p, jnp.add)
            acc = acc + _dot(p.astype(BF16), v_ref[chunk(c), hs])
            return l, acc

        l, acc = lax.fori_loop(0, n_ch, pv_body, (jnp.zeros((tq, LANES), F32), jnp.zeros((tq, HEAD_DIM), F32)))
        o_ref[:, hs] = (acc / jnp.sum(l, axis=1, keepdims=True)).astype(BF16)


def _attn_prompt(qb, qib, wi, kb, vb, kib, batch, seq, topk):
    n, d = qb.shape
    tq = min(ATTN_Q_TILE, seq)
    nq = seq // tq
    kc = min(ATTN_K_CHUNK, seq)
    tile_map = lambda b, j: (b * nq + j, 0)
    batch_map = lambda b, j: (b, 0)
    return pl.pallas_call(
        functools.partial(_attn_prompt_kernel, topk=topk),
        grid=(batch, nq),
        in_specs=[pl.BlockSpec((tq, d), tile_map), pl.BlockSpec((tq, d), tile_map),
                  pl.BlockSpec((tq, LANES), tile_map),
                  pl.BlockSpec((seq, d), batch_map, pipeline_mode=pl.Buffered(1)),
                  pl.BlockSpec((seq, d), batch_map, pipeline_mode=pl.Buffered(1)),
                  pl.BlockSpec((seq, IDX_DIM), batch_map, pipeline_mode=pl.Buffered(1))],
        out_specs=pl.BlockSpec((tq, d), tile_map),
        out_shape=jax.ShapeDtypeStruct((n, d), BF16),
        scratch_shapes=[pltpu.VMEM((tq, seq), F32), pltpu.VMEM((tq, seq), F32),
                        pltpu.VMEM((IDX_HEADS, tq, kc), F32)],
        compiler_params=_compiler_params(("parallel", "arbitrary")),
        name="attn_prompt",
    )(qb, qib, wi, kb, vb, kib)


def _sample_select_kernel(pt_ref, qi_ref, w_ref, kin_ref, *rest, topk):
    del pt_ref
    page_refs = rest[:IDX_PAGES_PER_STEP]
    sel_ref, selself_ref, score_ref = rest[IDX_PAGES_PER_STEP:]
    g = pl.program_id(1)
    n_pages = score_ref.shape[0]
    qi = qi_ref[...]
    w = w_ref[...]

    rows = []
    for r in range(IDX_PAGES_PER_STEP):
        d = lax.dot_general(qi, page_refs[r][...].astype(BF16), _NT, preferred_element_type=F32)
        rows.append(jnp.sum(w * jnp.maximum(d, 0.0), axis=0, keepdims=True))
    score_ref[pl.ds(pl.multiple_of(g * IDX_PAGES_PER_STEP, IDX_PAGES_PER_STEP), IDX_PAGES_PER_STEP), :] = (
        jnp.concatenate(rows, axis=0))

    @pl.when(g == pl.num_programs(1) - 1)
    def _():
        x = score_ref[...]
        d_self = jnp.sum(qi.astype(F32) * kin_ref[...].astype(BF16).astype(F32), axis=1, keepdims=True)
        x_self = jnp.sum(w[:, :1] * jnp.maximum(d_self, 0.0), axis=0, keepdims=True)

        def total(v, op):
            return op(op(v, axis=0, keepdims=True), axis=1, keepdims=True)

        def count_ge(p):
            return total(jnp.where(x >= p, 1.0, 0.0), jnp.sum) + jnp.where(x_self >= p, 1.0, 0.0)

        def max_below(hi):
            return jnp.maximum(total(jnp.where(x < hi, x, -jnp.inf), jnp.max),
                               jnp.where(x_self < hi, x_self, -jnp.inf))

        row_max = jnp.maximum(total(x, jnp.max), x_self)
        row_min = jnp.minimum(total(x, jnp.min), x_self)
        n_valid = jnp.full((1, 1), float(n_pages * PAGE_SIZE + 1), F32)
        kk = jnp.full((1, 1), float(topk), F32)
        lo, hi, clo, chi = _topk_bracket(count_ge, max_below, row_min, row_max, n_valid, kk)

        need = kk - chi
        cand = jnp.where((x >= lo) & (x < hi), 1.0, 0.0)
        ps = cand.shape[1]
        tri = jnp.where(lax.broadcasted_iota(jnp.int32, (ps, ps), 0) <= lax.broadcasted_iota(jnp.int32, (ps, ps), 1),
                        1.0, 0.0).astype(BF16)
        below = jnp.where(lax.broadcasted_iota(jnp.int32, (n_pages, n_pages), 1)
                          < lax.broadcasted_iota(jnp.int32, (n_pages, n_pages), 0), 1.0, 0.0).astype(BF16)
        page_tot = jnp.broadcast_to(jnp.sum(cand, axis=1, keepdims=True), cand.shape).astype(BF16)
        rank = _dot(cand.astype(BF16), tri) + _dot(below, page_tot)
        take = (cand > 0.0) & (rank <= need)
        sel = jnp.where((x >= hi) | take, 1.0, 0.0).astype(BF16)
        cols = sel_ref.shape[1]
        spread = jnp.where(lax.broadcasted_iota(jnp.int32, (ps, cols), 1) // N_HEADS
                           == lax.broadcasted_iota(jnp.int32, (ps, cols), 0), 1.0, 0.0).astype(BF16)
        sel_ref[...] = _dot(sel, spread)
        cand_self = (x_self >= lo) & (x_self < hi)
        take_self = cand_self & (total(cand, jnp.sum) + 1.0 <= need)
        selself_ref[...] = jnp.broadcast_to(jnp.where((x_self >= hi) | take_self, 1.0, 0.0), selself_ref.shape)


def _sample_select(qi8, w8, ki_new, cache_kidx, page_table, layer, topk):
    db, n_pages = page_table.shape
    steps = n_pages // IDX_PAGES_PER_STEP

    def page_spec(r):
        return pl.BlockSpec((None, None, PAGE_SIZE, IDX_DIM),
                            lambda b, g, pt: (layer, pt[b, g * IDX_PAGES_PER_STEP + r], 0, 0))

    per_seq = lambda b, g, pt: (b, 0, 0)
    cols = PAGE_SIZE * N_HEADS
    grid_spec = pltpu.PrefetchScalarGridSpec(
        num_scalar_prefetch=1,
        grid=(db, steps),
        in_specs=[pl.BlockSpec((None, IDX_HEADS, IDX_DIM), per_seq), pl.BlockSpec((None, IDX_HEADS, LANES), per_seq),
                  pl.BlockSpec((None, 1, IDX_DIM), per_seq)] + [page_spec(r) for r in range(IDX_PAGES_PER_STEP)],
        out_specs=[pl.BlockSpec((None, n_pages, cols), per_seq), pl.BlockSpec((None, 1, LANES), per_seq)],
        scratch_shapes=[pltpu.VMEM((n_pages, PAGE_SIZE), F32)],
    )
    return pl.pallas_call(
        functools.partial(_sample_select_kernel, topk=topk),
        grid_spec=grid_spec,
        out_shape=(jax.ShapeDtypeStruct((db, n_pages, cols), F32), jax.ShapeDtypeStruct((db, 1, LANES), F32)),
        compiler_params=_compiler_params(("parallel", "arbitrary")),
        name="sample_select",
    )(page_table, qi8, w8, ki_new, *([cache_kidx] * IDX_PAGES_PER_STEP))


def _sample_attn_kernel(pt_ref, q_ref, sel_ref, selself_ref, kn_ref, vn_ref, *rest):
    del pt_ref
    k_refs = rest[:KV_PAGES_PER_STEP]
    v_refs = rest[KV_PAGES_PER_STEP:2 * KV_PAGES_PER_STEP]
    o_ref, m_ref, l_ref, acc_ref = rest[2 * KV_PAGES_PER_STEP:]
    g = pl.program_id(1)
    q = q_ref[...]
    q_bf = q.astype(BF16)
    cols = PAGE_SIZE * N_HEADS
    own = (lax.broadcasted_iota(jnp.int32, (N_HEADS, cols), 1) % N_HEADS
           == lax.broadcasted_iota(jnp.int32, (N_HEADS, cols), 0))

    @pl.when(g == 0)
    def _():
        m_ref[...] = jnp.full(m_ref.shape, -jnp.inf, F32)
        l_ref[...] = jnp.zeros(l_ref.shape, F32)
        acc_ref[...] = jnp.zeros(acc_ref.shape, F32)

    def accumulate(s, value_fn):
        m_old = m_ref[...]
        m_new = jnp.maximum(m_old, jnp.max(s, axis=1, keepdims=True))
        m_safe = jnp.where(m_new == -jnp.inf, 0.0, m_new)
        alpha = jnp.exp(m_old - m_safe)
        p = jnp.exp(s - m_safe)
        l_ref[...] = alpha * l_ref[...] + jnp.sum(p, axis=1, keepdims=True)
        acc_ref[...] = alpha * acc_ref[...] + value_fn(p)
        m_ref[...] = m_new

    sel = sel_ref[...]
    for r in range(KV_PAGES_PER_STEP):
        s = lax.dot_general(q_bf, k_refs[r][...].astype(BF16), _NT, preferred_element_type=F32)
        s = jnp.where(own & (sel[r:r + 1, :] > 0.0), s, -jnp.inf)
        accumulate(s, lambda p, r=r: _dot(p.astype(BF16), v_refs[r][...].astype(BF16)))

    @pl.when(g == pl.num_programs(1) - 1)
    def _():
        kn = kn_ref[...].astype(BF16).astype(F32)
        s_self = jnp.sum(q * kn, axis=1, keepdims=True)
        s_self = jnp.where(selself_ref[:, :1] > 0.0, s_self, -jnp.inf)
        vn = vn_ref[...].astype(BF16).astype(F32)
        accumulate(s_self, lambda p: p.astype(BF16).astype(F32) * vn)
        o_ref[...] = acc_ref[...] / l_ref[...]


def _sample_attn(q, sel_cols, sel_self, k_new, v_new, cache_k, cache_v, page_table, layer):
    db, n_pages = page_table.shape
    steps = n_pages // KV_PAGES_PER_STEP
    cols = PAGE_SIZE * N_HEADS
    sel4 = sel_cols.reshape(db, steps, KV_PAGES_PER_STEP, cols)

    def page_spec(r):
        return pl.BlockSpec((None, None, cols, HEAD_DIM),
                            lambda b, g, pt: (layer, pt[b, g * KV_PAGES_PER_STEP + r], 0, 0))

    per_seq = lambda b, g, pt: (b, 0, 0)
    heads_spec = pl.BlockSpec((None, N_HEADS, HEAD_DIM), per_seq)
    grid_spec = pltpu.PrefetchScalarGridSpec(
        num_scalar_prefetch=1,
        grid=(db, steps),
        in_specs=[heads_spec,
                  pl.BlockSpec((None, None, KV_PAGES_PER_STEP, cols), lambda b, g, pt: (b, g, 0, 0)),
                  pl.BlockSpec((None, 1, LANES), per_seq), heads_spec, heads_spec]
                 + [page_spec(r) for r in range(KV_PAGES_PER_STEP)] * 2,
        out_specs=heads_spec,
        scratch_shapes=[pltpu.VMEM((N_HEADS, 1), F32), pltpu.VMEM((N_HEADS, 1), F32),
                        pltpu.VMEM((N_HEADS, HEAD_DIM), F32)],
    )
    return pl.pallas_call(
        _sample_attn_kernel,
        grid_spec=grid_spec,
        out_shape=jax.ShapeDtypeStruct((db, N_HEADS, HEAD_DIM), F32),
        compiler_params=_compiler_params(("parallel", "arbitrary")),
        name="sample_attn",
    )(page_table, q, sel4, sel_self, k_new, v_new, *([cache_k] * KV_PAGES_PER_STEP), *([cache_v] * KV_PAGES_PER_STEP))


def _ln_swish(y, g, b):
    yc = y - jnp.mean(y, axis=-1, keepdims=True)
    yn = yc * lax.rsqrt(jnp.mean(yc * yc, axis=-1, keepdims=True) + EPS) * g + b
    return yn * jax.nn.sigmoid(yn)


def _conv_prompt_kernel(cur_ref, halo_ref, w_ref, bdw_ref, lng_ref, lnb_ref, y_ref, ext_ref, conv_ref):
    tt, d = cur_ref.shape
    ext_ref[:CONV_HALO, :] = jnp.where(pl.program_id(1) == 0, 0.0, halo_ref[...])
    ext_ref[CONV_HALO:, :] = cur_ref[...]
    shift = CONV_HALO - CONV_CTX
    for rb in range(tt // CONV_ROW_BLOCK):
        r0 = rb * CONV_ROW_BLOCK
        for lt in range(d // LANES):
            ls = slice(lt * LANES, (lt + 1) * LANES)
            acc = jnp.zeros((CONV_ROW_BLOCK, LANES), F32)
            for j in range(CONV_WIDTH):
                acc = acc + w_ref[j:j + 1, ls] * ext_ref[r0 + j + shift:r0 + j + shift + CONV_ROW_BLOCK, ls]
            conv_ref[r0:r0 + CONV_ROW_BLOCK, ls] = acc
    y = conv_ref[...] + bdw_ref[...]
    y_ref[...] = _ln_swish(y, lng_ref[...], lnb_ref[...]).astype(BF16)


def _conv_prompt(glu, w_dw, b_dw, ln_g, ln_b, batch, seq):
    n, d = glu.shape
    tt = min(CONV_T_TILE, seq)
    nt = seq // tt
    halo_per_tile = tt // CONV_HALO
    halo_per_seq = seq // CONV_HALO
    tile_map = lambda b, j: (b * nt + j, 0)
    halo_map = lambda b, j: (jnp.maximum(b * halo_per_seq + j * halo_per_tile - 1, 0), 0)
    consts = (w_dw, b_dw, ln_g, ln_b)
    return pl.pallas_call(
        _conv_prompt_kernel,
        grid=(batch, nt),
        in_specs=[pl.BlockSpec((tt, d), tile_map), pl.BlockSpec((CONV_HALO, d), halo_map)]
                 + [_resident(c.shape) for c in consts],
        out_specs=pl.BlockSpec((tt, d), tile_map),
        out_shape=jax.ShapeDtypeStruct((n, d), BF16),
        scratch_shapes=[pltpu.VMEM((tt + CONV_HALO, d), F32), pltpu.VMEM((tt, d), F32)],
        compiler_params=_compiler_params(("parallel", "arbitrary")),
        name="conv_prompt",
    )(glu, glu, *consts)


def _conv_sample_kernel(state_ref, u_ref, w_ref, bdw_ref, lng_ref, lnb_ref, y_ref, conv_ref):
    db = u_ref.shape[0]
    w_ctx = w_ref[:CONV_CTX, :]
    for b in range(db):
        conv_ref[b:b + 1, :] = jnp.sum(state_ref[b] * w_ctx, axis=0, keepdims=True)
    y = conv_ref[...] + u_ref[...] * w_ref[CONV_CTX:CONV_CTX + 1, :] + bdw_ref[...]
    y_ref[...] = _ln_swish(y, lng_ref[...], lnb_ref[...]).astype(BF16)


def _conv_sample(state, u, w_dw, b_dw, ln_g, ln_b):
    db, d = u.shape
    args = (state, u, w_dw, b_dw, ln_g, ln_b)
    return pl.pallas_call(
        _conv_sample_kernel,
        grid=(1,),
        in_specs=[_resident(a.shape) for a in args],
        out_specs=pl.BlockSpec((db, d), lambda i: (0, 0)),
        out_shape=jax.ShapeDtypeStruct((db, d), BF16),
        scratch_shapes=[pltpu.VMEM((db, d), F32)],
        compiler_params=_compiler_params(("arbitrary",)),
        name="conv_sample",
    )(*args)


def kernel(x_prompt, x_sample, p_prompt, p_sample, cache_k, cache_v, cache_kidx, state_conv, page_table, ffn1_norm, ffn1_w_gate, ffn1_w_up, ffn1_w_down, mix_norm, attn_w_in, attn_q_gain, attn_k_gain, attn_kidx_gain, attn_w_o, conv_w_in, conv_b_in, conv_w_dw, conv_b_dw, conv_ln_g, conv_ln_b, conv_w_out, conv_b_out, ffn2_norm, ffn2_w_gate, ffn2_w_up, ffn2_w_down, ple_norm, ple_w_gate, ple_w_proj):
    batch, seq, d = x_prompt.shape
    db, t_new, _ = x_sample.shape
    depth = ffn1_norm.shape[0]
    n_mixers = 2
    n_pages = page_table.shape[1]
    past = n_pages * PAGE_SIZE
    attn_dim = N_HEADS * HEAD_DIM
    assert t_new == 1 and d == attn_dim
    topk_p = min(TOPK_MAX, seq // 4)
    topk_s = min(TOPK_MAX, (past + t_new) // 4)
    n_p = batch * seq
    tile_p = min(ROW_TILE, n_p)
    tile_s = db

    row = lambda v: v.reshape(1, -1)
    bf = lambda w: w.astype(BF16)
    hp = x_prompt.reshape(n_p, d)
    hs = x_sample.reshape(db, d)
    cache_k4 = cache_k.reshape(cache_k.shape[0], cache_k.shape[1], PAGE_SIZE * N_HEADS, HEAD_DIM)
    cache_v4 = cache_v.reshape(cache_v.shape[0], cache_v.shape[1], PAGE_SIZE * N_HEADS, HEAD_DIM)
    zero_bias = jnp.zeros((1, d), F32)

    k_p, v_p, ki_p, conv_p = [], [], [], []
    k_s, v_s, ki_s, conv_s = [], [], [], []
    for i in range(depth):
        ffn1 = (row(ffn1_norm[i]), bf(ffn1_w_gate[i]), bf(ffn1_w_up[i]), bf(ffn1_w_down[i]))
        ffn2 = (row(ffn2_norm[i]), bf(ffn2_w_gate[i]), bf(ffn2_w_up[i]), bf(ffn2_w_down[i]))
        ple = (row(ple_norm[i]), bf(ple_w_gate[i]), bf(ple_w_proj[i]))
        g_mix = row(mix_norm[i])
        if i % n_mixers == 0:
            a = i // n_mixers
            w_in = attn_w_in[a]
            o1, o2, o3 = attn_dim, 2 * attn_dim, 3 * attn_dim
            o4 = o3 + IDX_HEADS * IDX_DIM
            o5 = o4 + IDX_DIM
            w_wi = jnp.pad(w_in[:, o5:], ((0, 0), (0, LANES - IDX_HEADS)))
            proj = (bf(w_in[:, :o1]), bf(w_in[:, o1:o2]), bf(w_in[:, o2:o3]), bf(w_in[:, o3:o4]), bf(w_in[:, o4:o5]),
                    bf(w_wi), row(attn_q_gain[a]), row(attn_k_gain[a]), row(attn_kidx_gain[a]))
            w_out, b_out = bf(attn_w_o[a]), zero_bias

            hp, qb, kf, vf, kif, kb, vb, qib, kib, wi = _stage_a_attn(hp, ffn1, g_mix, proj, tile_p)
            mp = _attn_prompt(qb, qib, wi, kb, vb, kib, batch, seq, topk_p)
            k_p.append(kf.reshape(batch, seq, N_HEADS, HEAD_DIM))
            v_p.append(vf.reshape(batch, seq, N_HEADS, HEAD_DIM))
            ki_p.append(kif.reshape(batch, seq, IDX_DIM))

            hs, qb, kf, vf, kif, kb, vb, qib, kib, wi = _stage_a_attn(hs, ffn1, g_mix, proj, tile_s)
            qi8 = qib.reshape(db, IDX_HEADS, IDX_DIM)
            w8 = jnp.broadcast_to(wi[:, :IDX_HEADS, None], (db, IDX_HEADS, LANES))
            sel, sel_self = _sample_select(qi8, w8, kif.reshape(db, 1, IDX_DIM), cache_kidx, page_table, a, topk_s)
            per_head = lambda v: v.astype(F32).reshape(db, N_HEADS, HEAD_DIM)
            ms = _sample_attn(per_head(qb), sel, sel_self, per_head(kf), per_head(vf), cache_k4, cache_v4,
                              page_table, a)
            ms = ms.reshape(db, d).astype(BF16)
            k_s.append(kf.reshape(db, t_new, N_HEADS, HEAD_DIM))
            v_s.append(vf.reshape(db, t_new, N_HEADS, HEAD_DIM))
            ki_s.append(kif.reshape(db, t_new, IDX_DIM))
        else:
            c = i // n_mixers
            w_dw = jnp.pad(conv_w_dw[c], ((0, CONV_HALO - CONV_WIDTH), (0, 0)))
            conv = (w_dw, row(conv_b_dw[c]), row(conv_ln_g[c]), row(conv_ln_b[c]))
            w_out, b_out = bf(conv_w_out[c]), row(conv_b_out[c])

            hp, glu = _stage_a_conv(hp, ffn1, g_mix, bf(conv_w_in[c]), row(conv_b_in[c]), tile_p)
            mp = _conv_prompt(glu, *conv, batch, seq)
            conv_p.append(glu.reshape(batch, seq, d)[:, seq - CONV_CTX:])

            hs, glu = _stage_a_conv(hs, ffn1, g_mix, bf(conv_w_in[c]), row(conv_b_in[c]), tile_s)
            ms = _conv_sample(state_conv[c], glu, *conv)
            conv_s.append(jnp.concatenate([state_conv[c], glu[:, None, :]], axis=1)[:, -CONV_CTX:])

        hp = _stage_c(hp, mp, p_prompt[i].reshape(n_p, -1), w_out, b_out, ffn2, ple, tile_p)
        hs = _stage_c(hs, ms, p_sample[i].reshape(db, -1), w_out, b_out, ffn2, ple, tile_s)

    return (hp.reshape(batch, seq, d), hs.reshape(db, t_new, d), jnp.stack(k_p), jnp.stack(v_p), jnp.stack(ki_p),
            jnp.stack(conv_p), jnp.stack(k_s), jnp.stack(v_s), jnp.stack(ki_s), jnp.stack(conv_s))
```

```python
import functools

import jax
import jax.numpy as jnp
from jax import lax
from jax.experimental import pallas as pl
from jax.experimental.pallas import tpu as pltpu

F32 = jnp.float32
BF16 = jnp.bfloat16

N_HEADS = 8
HEAD_DIM = 128
IDX_HEADS = 8
IDX_DIM = 128
TOPK_MAX = 256
PAGE_SIZE = 128
CONV_WIDTH = 31
CONV_CTX = CONV_WIDTH - 1
EPS = 1e-6

LANES = 128
SUBLANES = 8
VMEM_LIMIT_BYTES = 56 * 1024 * 1024
ROW_TILE = 512
FF_CHUNK = 256
ATTN_Q_TILE = 256
ATTN_K_CHUNK = 512
CONV_T_TILE = 256
CONV_HALO = 32
CONV_ROW_BLOCK = 32
N_BISECT = 18
FOLD_ROWS = 32
LOG2_E = 1.4426950408889634
V_ROWS = HEAD_DIM + 16
BISECT_STEPS_PER_CHECK = 3
IDX_PAGES_PER_STEP = 16
KV_PAGES_PER_STEP = 8

_NT = (((1,), (1,)), ((), ()))


def _compiler_params(semantics):
    return pltpu.CompilerParams(dimension_semantics=semantics, vmem_limit_bytes=VMEM_LIMIT_BYTES)


def _resident(shape):
    nd = len(shape)
    return pl.BlockSpec(shape, lambda *_: (0,) * nd, pipeline_mode=pl.Buffered(1))


def _rows(tile, cols):
    return pl.BlockSpec((tile, cols), lambda i: (i, 0))


def _rms(x, g):
    return x * lax.rsqrt(jnp.mean(x * x, axis=-1, keepdims=True) + EPS) * g


def _dot(a, b):
    return jnp.dot(a, b, preferred_element_type=F32)


def _dot_nt(a, b):
    return lax.dot_general(a, b, _NT, preferred_element_type=F32)


def _ffn_half_step(x, g_ref, wg_ref, wu_ref, wd_ref, act_ref):
    hn = _rms(x, g_ref[...]).astype(BF16)
    d_ff = wg_ref.shape[1]
    for c in range(d_ff // FF_CHUNK):
        sl = slice(c * FF_CHUNK, (c + 1) * FF_CHUNK)
        g = _dot(hn, wg_ref[:, sl])
        u = _dot(hn, wu_ref[:, sl])
        act_ref[:, sl] = (g * jax.nn.sigmoid(g) * u).astype(BF16)
    return x + 0.5 * _dot(act_ref[...], wd_ref[...])


def _stage_a_attn_kernel(x_ref, g1_ref, wg_ref, wu_ref, wd_ref, gm_ref, wq_ref, wk_ref, wv_ref, wvt_ref, wqi_ref,
                         wki_ref, wwi_ref, qg_ref, kg_ref, kig_ref,
                         h_ref, qb_ref, kf_ref, vf_ref, kif_ref, kb_ref, vt_ref, qib_ref, kib_ref, wi_ref,
                         act_ref):
    h = _ffn_half_step(x_ref[...], g1_ref, wg_ref, wu_ref, wd_ref, act_ref)
    h_ref[...] = h
    hn = _rms(h, gm_ref[...]).astype(BF16)

    zq = _dot(hn, wq_ref[...])
    for hh in range(N_HEADS):
        sl = slice(hh * HEAD_DIM, (hh + 1) * HEAD_DIM)
        qb_ref[:, sl] = (_rms(zq[:, sl], qg_ref[...]) * (HEAD_DIM ** -0.5 * LOG2_E)).astype(BF16)
    zk = _dot(hn, wk_ref[...])
    for hh in range(N_HEADS):
        sl = slice(hh * HEAD_DIM, (hh + 1) * HEAD_DIM)
        kn = _rms(zk[:, sl], kg_ref[...])
        kf_ref[:, sl] = kn
        kb_ref[:, sl] = kn.astype(BF16)
    vf_ref[...] = _dot(hn, wv_ref[...])
    vt = _dot_nt(wvt_ref[...], hn).astype(BF16)
    for hh in range(N_HEADS):
        vt_ref[hh * V_ROWS:hh * V_ROWS + HEAD_DIM, :] = vt[hh * HEAD_DIM:(hh + 1) * HEAD_DIM, :]
        vt_ref[hh * V_ROWS + HEAD_DIM:(hh + 1) * V_ROWS, :] = jnp.ones((V_ROWS - HEAD_DIM, vt.shape[1]), BF16)
    qib_ref[...] = _dot(hn, wqi_ref[...]).astype(BF16)
    kin = _rms(_dot(hn, wki_ref[...]), kig_ref[...])
    kif_ref[...] = kin
    kib_ref[...] = kin.astype(BF16)
    wi_ref[...] = _dot(hn, wwi_ref[...])


def _stage_a_attn(x, ffn, g_mix, proj, tile, batch):
    n, d = x.shape
    g1, wg, wu, wd = ffn
    wq, wk, wv, wvt, wqi, wki, wwi, qg, kg, kig = proj
    d_ff = wg.shape[1]
    consts = (g1, wg, wu, wd, g_mix, wq, wk, wv, wvt, wqi, wki, wwi, qg, kg, kig)
    sds = jax.ShapeDtypeStruct
    seq = n // batch
    tiles_per_seq = seq // tile
    out_shape = (sds((n, d), F32), sds((n, d), BF16), sds((n, d), F32), sds((n, d), F32),
                 sds((n, IDX_DIM), F32), sds((n, d), BF16), sds((batch, N_HEADS * V_ROWS, seq), BF16), sds((n, d), BF16),
                 sds((n, IDX_DIM), BF16), sds((n, LANES), F32))
    out_specs = [_rows(tile, s.shape[1]) for s in out_shape]
    out_specs[6] = pl.BlockSpec((None, N_HEADS * V_ROWS, tile),
                                lambda i: (i // tiles_per_seq, 0, i % tiles_per_seq))
    return pl.pallas_call(
        _stage_a_attn_kernel,
        grid=(n // tile,),
        in_specs=[_rows(tile, d)] + [_resident(c.shape) for c in consts],
        out_specs=out_specs,
        out_shape=out_shape,
        scratch_shapes=[pltpu.VMEM((tile, d_ff), BF16)],
        compiler_params=_compiler_params(("parallel",)),
        name="stage_a_attn",
    )(x, *consts)


def _stage_a_conv_kernel(x_ref, g1_ref, wg_ref, wu_ref, wd_ref, gm_ref, wci_ref, bci_ref,
                         h_ref, glu_ref, act_ref):
    h = _ffn_half_step(x_ref[...], g1_ref, wg_ref, wu_ref, wd_ref, act_ref)
    h_ref[...] = h
    hn = _rms(h, gm_ref[...]).astype(BF16)
    d = h.shape[1]
    u = _dot(hn, wci_ref[...]) + bci_ref[...]
    glu_ref[...] = u[:, :d] * jax.nn.sigmoid(u[:, d:])


def _stage_a_conv(x, ffn, g_mix, wci, bci, tile):
    n, d = x.shape
    g1, wg, wu, wd = ffn
    consts = (g1, wg, wu, wd, g_mix, wci, bci)
    out_shape = (jax.ShapeDtypeStruct((n, d), F32), jax.ShapeDtypeStruct((n, d), F32))
    return pl.pallas_call(
        _stage_a_conv_kernel,
        grid=(n // tile,),
        in_specs=[_rows(tile, d)] + [_resident(c.shape) for c in consts],
        out_specs=[_rows(tile, d), _rows(tile, d)],
        out_shape=out_shape,
        scratch_shapes=[pltpu.VMEM((tile, wg.shape[1]), BF16)],
        compiler_params=_compiler_params(("parallel",)),
        name="stage_a_conv",
    )(x, *consts)


def _stage_c_kernel(h_ref, m_ref, p_ref, wo_ref, bo_ref, g2_ref, wg_ref, wu_ref, wd_ref, gp_ref, wpg_ref,
                    wpp_ref, out_ref, act_ref):
    h = h_ref[...] + _dot(m_ref[...], wo_ref[...]) + bo_ref[...]
    h = _ffn_half_step(h, g2_ref, wg_ref, wu_ref, wd_ref, act_ref)
    gate = jax.nn.sigmoid(_dot(_rms(h, gp_ref[...]).astype(BF16), wpg_ref[...]))
    out_ref[...] = h + gate * _dot(p_ref[...].astype(BF16), wpp_ref[...])


def _stage_c(h, m, p, wo, bo, ffn, ple, tile):
    n, d = h.shape
    g2, wg, wu, wd = ffn
    gp, wpg, wpp = ple
    consts = (wo, bo, g2, wg, wu, wd, gp, wpg, wpp)
    return pl.pallas_call(
        _stage_c_kernel,
        grid=(n // tile,),
        in_specs=[_rows(tile, d), _rows(tile, d), _rows(tile, p.shape[1])] + [_resident(c.shape) for c in consts],
        out_specs=_rows(tile, d),
        out_shape=jax.ShapeDtypeStruct((n, d), F32),
        scratch_shapes=[pltpu.VMEM((tile, wg.shape[1]), BF16)],
        compiler_params=_compiler_params(("parallel",)),
        name="stage_c",
    )(h, m, p, *consts)


def _any(flag):
    return jnp.max(jnp.where(flag, 1.0, 0.0)).astype(jnp.int32)


def _topk_bracket(count_ge, max_below, row_min, row_max, n_valid, kk):
    def update(lo, hi, clo, chi, p, c):
        pend = clo != kk
        up = pend & (c >= kk)
        dn = pend & (c < kk)
        return (jnp.where(up, p, lo), jnp.where(dn, p, hi), jnp.where(up, c, clo), jnp.where(dn, c, chi))

    state = update(row_min, jnp.full_like(row_min, jnp.inf), n_valid, jnp.zeros_like(row_min),
                   row_max, count_ge(row_max))

    def bisect_cond(carry):
        return (carry[0] < N_BISECT) & (carry[1] > 0)

    def bisect_body(carry):
        it, _, *st = carry
        for _ in range(BISECT_STEPS_PER_CHECK):
            p = 0.5 * st[0] + 0.5 * st[1]
            st = update(*st, p, count_ge(p))
        return (it + BISECT_STEPS_PER_CHECK, _any(st[2] != kk)) + tuple(st)

    carry = lax.while_loop(bisect_cond, bisect_body, (jnp.int32(0), _any(state[2] != kk)) + state)
    lo, hi, clo, chi = carry[2:]
    tied = jnp.zeros_like(lo)

    def peel_cond(carry):
        return carry[0] > 0

    def peel_body(carry):
        _, lo, hi, clo, chi, tied = carry
        m = max_below(hi)
        c = count_ge(m)
        pend = (clo != kk) & (tied == 0.0)
        fin = pend & (c >= kk)
        dn = pend & (c < kk)
        lo = jnp.where(fin, m, lo)
        clo = jnp.where(fin, c, clo)
        tied = jnp.where(fin, 1.0, tied)
        hi = jnp.where(dn, m, hi)
        chi = jnp.where(dn, c, chi)
        return (_any((clo != kk) & (tied == 0.0)), lo, hi, clo, chi, tied)

    carry = lax.while_loop(peel_cond, peel_body, (_any(clo != kk), lo, hi, clo, chi, tied))
    return carry[1:5]


def _fold_rows(x, op):
    return op(x.reshape(x.shape[0] // FOLD_ROWS, FOLD_ROWS, x.shape[1]), axis=0)


def _attn_prompt_kernel(q_ref, qi_ref, wi_ref, k_ref, vt_ref, ki_ref, o_ref, score_ref, wt_ref, m_ref, acc_ref,
                        s_ref, *, topk):
    tq = q_ref.shape[0]
    seq = k_ref.shape[0]
    kc = min(ATTN_K_CHUNK, seq)
    q0 = pl.program_id(1) * tq
    n_ch = (q0 + tq + kc - 1) // kc
    q_pos = q0 + lax.broadcasted_iota(jnp.int32, (1, tq), 1)

    def chunk(c):
        return pl.ds(pl.multiple_of(c * kc, kc), kc)

    wt_ref[...] = wi_ref[...].T

    def score_body(c, carry):
        mx, mn = carry
        kic = ki_ref[chunk(c), :]
        acc = jnp.zeros((kc, tq), F32)
        for hh in range(IDX_HEADS):
            d = _dot_nt(kic, qi_ref[:, hh * IDX_DIM:(hh + 1) * IDX_DIM])
            acc = acc + wt_ref[hh:hh + 1, :] * jnp.maximum(d, 0.0)
        key_pos = c * kc + lax.broadcasted_iota(jnp.int32, (kc, 1), 0)
        allowed = key_pos <= q_pos
        sc = jnp.where(allowed, acc, -jnp.inf)
        score_ref[chunk(c), :] = sc
        mx = jnp.maximum(mx, _fold_rows(sc, jnp.max))
        mn = jnp.minimum(mn, _fold_rows(jnp.where(allowed, acc, jnp.inf), jnp.min))
        return mx, mn

    mx, mn = lax.fori_loop(0, n_ch, score_body,
                           (jnp.full((FOLD_ROWS, tq), -jnp.inf, F32), jnp.full((FOLD_ROWS, tq), jnp.inf, F32)))
    row_max = jnp.max(mx, axis=0, keepdims=True)
    row_min = jnp.min(mn, axis=0, keepdims=True)

    def count_ge(p):
        def body(c, acc):
            return acc + _fold_rows(jnp.where(score_ref[chunk(c), :] >= p, 1.0, 0.0), jnp.sum)
        acc = lax.fori_loop(0, n_ch, body, jnp.zeros((FOLD_ROWS, tq), F32))
        return jnp.sum(acc, axis=0, keepdims=True)

    def max_below(hi):
        def body(c, acc):
            x = score_ref[chunk(c), :]
            return jnp.maximum(acc, _fold_rows(jnp.where(x < hi, x, -jnp.inf), jnp.max))
        acc = lax.fori_loop(0, n_ch, body, jnp.full((FOLD_ROWS, tq), -jnp.inf, F32))
        return jnp.max(acc, axis=0, keepdims=True)

    n_valid = (q_pos + 1).astype(F32)
    kk = jnp.minimum(n_valid, float(topk))
    lo, hi, clo, chi = _topk_bracket(count_ge, max_below, row_min, row_max, n_valid, kk)
    has_ties = _any(clo != kk) > 0

    @pl.when(jnp.logical_not(has_ties))
    def _():
        def body(c, carry):
            score_ref[chunk(c), :] = jnp.where(score_ref[chunk(c), :] >= lo, 0.0, -jnp.inf)
            return carry
        lax.fori_loop(0, n_ch, body, 0)

    @pl.when(has_ties)
    def _():
        need = kk - chi
        upto = jnp.where(lax.broadcasted_iota(jnp.int32, (kc, kc), 1) <= lax.broadcasted_iota(jnp.int32, (kc, kc), 0),
                         1.0, 0.0).astype(BF16)

        def body(c, before):
            x = score_ref[chunk(c), :]
            cand = jnp.where((x >= lo) & (x < hi), 1.0, 0.0)
            rank = before + _dot(upto, cand.astype(BF16))
            take = (cand > 0.0) & (rank <= need)
            score_ref[chunk(c), :] = jnp.where((x >= hi) | take, 0.0, -jnp.inf)
            return before + jnp.sum(cand, axis=0, keepdims=True)
        lax.fori_loop(0, n_ch, body, jnp.zeros((1, tq), F32))

    m_ref[...] = jnp.full(m_ref.shape, -jnp.inf, F32)
    acc_ref[...] = jnp.zeros(acc_ref.shape, F32)

    def attn_body(c, carry):
        bias = score_ref[chunk(c), :]
        m_loc = []
        for hh in range(N_HEADS):
            hs = slice(hh * HEAD_DIM, (hh + 1) * HEAD_DIM)
            s = _dot_nt(k_ref[chunk(c), hs], q_ref[:, hs]) + bias
            s_ref[hh] = s
            m_loc.append(jnp.max(s, axis=0, keepdims=True))
        m_old = m_ref[...]
        m_new = jnp.maximum(m_old, jnp.concatenate(m_loc, axis=0))
        m_safe = jnp.where(m_new == -jnp.inf, 0.0, m_new)
        alpha = jnp.exp2(m_old - m_safe)
        m_ref[...] = m_new
        for hh in range(N_HEADS):
            vs = slice(hh * V_ROWS, (hh + 1) * V_ROWS)
            p = jnp.exp2(s_ref[hh] - m_safe[hh:hh + 1, :])
            acc_ref[vs, :] = alpha[hh:hh + 1, :] * acc_ref[vs, :] + _dot(vt_ref[vs, chunk(c)], p.astype(BF16))
        return carry

    lax.fori_loop(0, n_ch, attn_body, 0)
    for hh in range(N_HEADS):
        r0 = hh * V_ROWS
        out = acc_ref[r0:r0 + HEAD_DIM, :] / acc_ref[r0 + HEAD_DIM:r0 + HEAD_DIM + 1, :]
        o_ref[:, hh * HEAD_DIM:(hh + 1) * HEAD_DIM] = out.T.astype(BF16)


def _attn_prompt(qb, qib, wi, kb, vt, kib, batch, seq, topk):
    n, d = qb.shape
    tq = min(ATTN_Q_TILE, seq)
    kc = min(ATTN_K_CHUNK, seq)
    nq = seq // tq
    tile_map = lambda b, j: (b * nq + j, 0)
    batch_map = lambda b, j: (b, 0)
    return pl.pallas_call(
        functools.partial(_attn_prompt_kernel, topk=topk),
        grid=(batch, nq),
        in_specs=[pl.BlockSpec((tq, d), tile_map), pl.BlockSpec((tq, d), tile_map),
                  pl.BlockSpec((tq, LANES), tile_map),
                  pl.BlockSpec((seq, d), batch_map, pipeline_mode=pl.Buffered(1)),
                  pl.BlockSpec((None, N_HEADS * V_ROWS, seq), lambda b, j: (b, 0, 0), pipeline_mode=pl.Buffered(1)),
                  pl.BlockSpec((seq, IDX_DIM), batch_map, pipeline_mode=pl.Buffered(1))],
        out_specs=pl.BlockSpec((tq, d), tile_map),
        out_shape=jax.ShapeDtypeStruct((n, d), BF16),
        scratch_shapes=[pltpu.VMEM((seq, tq), F32), pltpu.VMEM((LANES, tq), F32), pltpu.VMEM((N_HEADS, tq), F32),
                        pltpu.VMEM((N_HEADS * V_ROWS, tq), F32), pltpu.VMEM((N_HEADS, kc, tq), F32)],
        compiler_params=_compiler_params(("parallel", "arbitrary")),
        name="attn_prompt",
    )(qb, qib, wi, kb, vt, kib)


def _sample_scores_kernel(pt_ref, qi_ref, w_ref, kin_ref, *rest, pages_per_step):
    del pt_ref
    page_refs = rest[:pages_per_step]
    score_ref, self_ref = rest[pages_per_step:]
    qi = qi_ref[...]
    w = w_ref[...]

    rows = []
    for r in range(pages_per_step):
        d = _dot_nt(qi, page_refs[r][...].astype(BF16))
        rows.append(jnp.sum(w * jnp.maximum(d, 0.0), axis=0, keepdims=True))
    score_ref[...] = jnp.concatenate(rows, axis=0)

    @pl.when(pl.program_id(1) == 0)
    def _():
        d_self = jnp.sum(qi.astype(F32) * kin_ref[...].astype(BF16).astype(F32), axis=1, keepdims=True)
        self_ref[...] = jnp.sum(w * jnp.maximum(d_self, 0.0), axis=0, keepdims=True)


def _sample_scores(qi8, w8, ki_new, cache_kidx, page_table, layer):
    db, n_pages = page_table.shape
    pps = min(IDX_PAGES_PER_STEP, n_pages)
    steps = n_pages // pps

    def page_spec(r):
        return pl.BlockSpec((None, None, PAGE_SIZE, IDX_DIM), lambda b, g, pt: (layer, pt[b, g * pps + r], 0, 0))

    per_seq = lambda b, g, pt: (b, 0, 0)
    grid_spec = pltpu.PrefetchScalarGridSpec(
        num_scalar_prefetch=1,
        grid=(db, steps),
        in_specs=[pl.BlockSpec((None, IDX_HEADS, IDX_DIM), per_seq), pl.BlockSpec((None, IDX_HEADS, LANES), per_seq),
                  pl.BlockSpec((None, 1, IDX_DIM), per_seq)] + [page_spec(r) for r in range(pps)],
        out_specs=[pl.BlockSpec((None, pps, PAGE_SIZE), lambda b, g, pt: (b, g, 0)),
                   pl.BlockSpec((None, 1, LANES), per_seq)],
    )
    return pl.pallas_call(
        functools.partial(_sample_scores_kernel, pages_per_step=pps),
        grid_spec=grid_spec,
        out_shape=(jax.ShapeDtypeStruct((db, n_pages, PAGE_SIZE), F32), jax.ShapeDtypeStruct((db, 1, LANES), F32)),
        compiler_params=_compiler_params(("parallel", "arbitrary")),
        name="sample_scores",
    )(page_table, qi8, w8, ki_new, *([cache_kidx] * pps))


def _sample_select_kernel(x_ref, xself_ref, sel_ref, selself_ref, *, topk):
    x = x_ref[...]
    x_self = xself_ref[...][:, :, :1]
    db, n_pages, ps = x.shape

    def total(v, op):
        return op(op(v, axis=1, keepdims=True), axis=2, keepdims=True)

    def count_ge(p):
        return total(jnp.where(x >= p, 1.0, 0.0), jnp.sum) + jnp.where(x_self >= p, 1.0, 0.0)

    def max_below(hi):
        return jnp.maximum(total(jnp.where(x < hi, x, -jnp.inf), jnp.max), jnp.where(x_self < hi, x_self, -jnp.inf))

    row_max = jnp.maximum(total(x, jnp.max), x_self)
    row_min = jnp.minimum(total(x, jnp.min), x_self)
    n_valid = jnp.full((db, 1, 1), float(n_pages * ps + 1), F32)
    kk = jnp.full((db, 1, 1), float(topk), F32)
    lo, hi, clo, chi = _topk_bracket(count_ge, max_below, row_min, row_max, n_valid, kk)
    has_ties = _any(clo != kk) > 0

    @pl.when(jnp.logical_not(has_ties))
    def _():
        sel_ref[...] = jnp.where(x >= lo, 1.0, 0.0)
        selself_ref[...] = jnp.broadcast_to(jnp.where(x_self >= lo, 1.0, 0.0), selself_ref.shape)

    @pl.when(has_ties)
    def _():
        need = kk - chi
        cand = jnp.where((x >= lo) & (x < hi), 1.0, 0.0)
        upto = jnp.where(lax.broadcasted_iota(jnp.int32, (ps, ps), 0) <= lax.broadcasted_iota(jnp.int32, (ps, ps), 1),
                         1.0, 0.0).astype(BF16)
        below = jnp.where(lax.broadcasted_iota(jnp.int32, (n_pages, n_pages), 1)
                          < lax.broadcasted_iota(jnp.int32, (n_pages, n_pages), 0), 1.0, 0.0).astype(BF16)
        in_page = _dot(cand.reshape(db * n_pages, ps).astype(BF16), upto).reshape(db, n_pages, ps)
        page_tot = jnp.broadcast_to(jnp.sum(cand, axis=2, keepdims=True), cand.shape).astype(BF16)
        for b in range(db):
            rank = in_page[b] + _dot(below, page_tot[b])
            take = (cand[b] > 0.0) & (rank <= need[b])
            sel_ref[b] = jnp.where((x[b] >= hi[b]) | take, 1.0, 0.0)
        cand_self = (x_self >= lo) & (x_self < hi)
        take_self = cand_self & (total(cand, jnp.sum) + 1.0 <= need)
        selself_ref[...] = jnp.broadcast_to(jnp.where((x_self >= hi) | take_self, 1.0, 0.0), selself_ref.shape)


def _sample_select(scores, score_self, topk):
    args = (scores, score_self)
    return pl.pallas_call(
        functools.partial(_sample_select_kernel, topk=topk),
        grid=(1,),
        in_specs=[_resident(a.shape) for a in args],
        out_specs=[pl.BlockSpec(a.shape, lambda i: (0, 0, 0)) for a in args],
        out_shape=tuple(jax.ShapeDtypeStruct(a.shape, F32) for a in args),
        compiler_params=_compiler_params(("arbitrary",)),
        name="sample_select",
    )(*args)


def _sample_attn_kernel(pt_ref, q_ref, sel_ref, selself_ref, kn_ref, vn_ref, spread_ref, *rest, pages_per_step):
    del pt_ref
    k_refs = rest[:pages_per_step]
    v_refs = rest[pages_per_step:2 * pages_per_step]
    o_ref, m_ref, l_ref, acc_ref = rest[2 * pages_per_step:]
    g = pl.program_id(1)
    q = q_ref[...]
    q_bf = q.astype(BF16)
    cols = PAGE_SIZE * N_HEADS
    own = (lax.broadcasted_iota(jnp.int32, (N_HEADS, cols), 1) % N_HEADS
           == lax.broadcasted_iota(jnp.int32, (N_HEADS, cols), 0))

    @pl.when(g == 0)
    def _():
        m_ref[...] = jnp.full(m_ref.shape, -jnp.inf, F32)
        l_ref[...] = jnp.zeros(l_ref.shape, F32)
        acc_ref[...] = jnp.zeros(acc_ref.shape, F32)

    def accumulate(logits, value_fns):
        m_old = m_ref[...]
        m_new = m_old
        for s in logits:
            m_new = jnp.maximum(m_new, jnp.max(s, axis=1, keepdims=True))
        m_safe = jnp.where(m_new == -jnp.inf, 0.0, m_new)
        alpha = jnp.exp2(m_old - m_safe)
        l = alpha * l_ref[...]
        acc = alpha * acc_ref[...]
        for s, value_fn in zip(logits, value_fns):
            p = jnp.exp2(s - m_safe)
            l = l + jnp.sum(p, axis=1, keepdims=True)
            acc = acc + value_fn(p)
        l_ref[...] = l
        acc_ref[...] = acc
        m_ref[...] = m_new

    sel_cols = _dot(sel_ref[...].astype(BF16), spread_ref[...])
    logits = []
    for r in range(pages_per_step):
        s = _dot_nt(q_bf, k_refs[r][...].astype(BF16))
        logits.append(jnp.where(own & (sel_cols[r:r + 1, :] > 0.0), s, -jnp.inf))
    accumulate(logits, [lambda p, r=r: _dot(p.astype(BF16), v_refs[r][...].astype(BF16))
                        for r in range(pages_per_step)])

    @pl.when(g == pl.num_programs(1) - 1)
    def _():
        kn = kn_ref[...].astype(BF16).astype(F32)
        s_self = jnp.sum(q * kn, axis=1, keepdims=True)
        s_self = jnp.where(selself_ref[:, :1] > 0.0, s_self, -jnp.inf)
        vn = vn_ref[...].astype(BF16).astype(F32)
        accumulate([s_self], [lambda p: p.astype(BF16).astype(F32) * vn])
        o_ref[...] = acc_ref[...] / l_ref[...]


def _sample_attn(q, sel, sel_self, k_new, v_new, cache_k, cache_v, page_table, layer):
    db, n_pages = page_table.shape
    pps = min(KV_PAGES_PER_STEP, n_pages)
    steps = n_pages // pps
    cols = PAGE_SIZE * N_HEADS
    sel4 = sel.reshape(db, steps, pps, PAGE_SIZE)
    spread = (jnp.arange(cols)[None, :] // N_HEADS == jnp.arange(PAGE_SIZE)[:, None]).astype(BF16)

    def page_spec(r):
        return pl.BlockSpec((None, None, cols, HEAD_DIM), lambda b, g, pt: (layer, pt[b, g * pps + r], 0, 0))

    per_seq = lambda b, g, pt: (b, 0, 0)
    heads_spec = pl.BlockSpec((None, N_HEADS, HEAD_DIM), per_seq)
    grid_spec = pltpu.PrefetchScalarGridSpec(
        num_scalar_prefetch=1,
        grid=(db, steps),
        in_specs=[heads_spec,
                  pl.BlockSpec((None, None, pps, PAGE_SIZE), lambda b, g, pt: (b, g, 0, 0)),
                  pl.BlockSpec((None, 1, LANES), per_seq), heads_spec, heads_spec,
                  pl.BlockSpec((PAGE_SIZE, cols), lambda b, g, pt: (0, 0))]
                 + [page_spec(r) for r in range(pps)] * 2,
        out_specs=heads_spec,
        scratch_shapes=[pltpu.VMEM((N_HEADS, 1), F32), pltpu.VMEM((N_HEADS, 1), F32),
                        pltpu.VMEM((N_HEADS, HEAD_DIM), F32)],
    )
    return pl.pallas_call(
        functools.partial(_sample_attn_kernel, pages_per_step=pps),
        grid_spec=grid_spec,
        out_shape=jax.ShapeDtypeStruct((db, N_HEADS, HEAD_DIM), F32),
        compiler_params=_compiler_params(("parallel", "arbitrary")),
        name="sample_attn",
    )(page_table, q, sel4, sel_self, k_new, v_new, spread, *([cache_k] * pps), *([cache_v] * pps))


def _ln_swish(y, g, b):
    yc = y - jnp.mean(y, axis=-1, keepdims=True)
    yn = yc * lax.rsqrt(jnp.mean(yc * yc, axis=-1, keepdims=True) + EPS) * g + b
    return yn * jax.nn.sigmoid(yn)


def _conv_prompt_kernel(cur_ref, halo_ref, w_ref, bdw_ref, lng_ref, lnb_ref, y_ref, sh_ref, conv_ref):
    tt, d = cur_ref.shape
    sh_ref[0, :CONV_HALO, :] = jnp.where(pl.program_id(1) == 0, 0.0, halo_ref[...])
    sh_ref[0, CONV_HALO:, :] = cur_ref[...]
    n_sh = tt + CONV_HALO - SUBLANES
    for r in range(1, SUBLANES):
        sh_ref[r, :n_sh, :] = sh_ref[0, r:r + n_sh, :]
    first = CONV_HALO - CONV_CTX

    def block(rb, carry):
        r0 = pl.multiple_of(rb * CONV_ROW_BLOCK, CONV_ROW_BLOCK)
        for lt in range(d // LANES):
            ls = slice(lt * LANES, (lt + 1) * LANES)
            acc = jnp.zeros((CONV_ROW_BLOCK, LANES), F32)
            for j in range(CONV_WIDTH):
                a, r = divmod(j + first, SUBLANES)
                acc = acc + w_ref[j:j + 1, ls] * sh_ref[r, pl.ds(r0 + a * SUBLANES, CONV_ROW_BLOCK), ls]
            conv_ref[pl.ds(r0, CONV_ROW_BLOCK), ls] = acc
        return carry

    lax.fori_loop(0, tt // CONV_ROW_BLOCK, block, 0)
    y = conv_ref[...] + bdw_ref[...]
    y_ref[...] = _ln_swish(y, lng_ref[...], lnb_ref[...]).astype(BF16)


def _conv_prompt(glu, w_dw, b_dw, ln_g, ln_b, batch, seq):
    n, d = glu.shape
    tt = min(CONV_T_TILE, seq)
    nt = seq // tt
    halo_per_tile = tt // CONV_HALO
    halo_per_seq = seq // CONV_HALO
    tile_map = lambda b, j: (b * nt + j, 0)
    halo_map = lambda b, j: (jnp.maximum(b * halo_per_seq + j * halo_per_tile - 1, 0), 0)
    consts = (w_dw, b_dw, ln_g, ln_b)
    return pl.pallas_call(
        _conv_prompt_kernel,
        grid=(batch, nt),
        in_specs=[pl.BlockSpec((tt, d), tile_map), pl.BlockSpec((CONV_HALO, d), halo_map)]
                 + [_resident(c.shape) for c in consts],
        out_specs=pl.BlockSpec((tt, d), tile_map),
        out_shape=jax.ShapeDtypeStruct((n, d), BF16),
        scratch_shapes=[pltpu.VMEM((SUBLANES, tt + CONV_HALO, d), F32), pltpu.VMEM((tt, d), F32)],
        compiler_params=_compiler_params(("parallel", "arbitrary")),
        name="conv_prompt",
    )(glu, glu, *consts)


def _conv_sample_kernel(state_ref, u_ref, w_ref, bdw_ref, lng_ref, lnb_ref, y_ref, conv_ref):
    db = u_ref.shape[0]
    w_ctx = w_ref[:CONV_CTX, :]
    for b in range(db):
        conv_ref[b:b + 1, :] = jnp.sum(state_ref[b] * w_ctx, axis=0, keepdims=True)
    y = conv_ref[...] + u_ref[...] * w_ref[CONV_CTX:CONV_CTX + 1, :] + bdw_ref[...]
    y_ref[...] = _ln_swish(y, lng_ref[...], lnb_ref[...]).astype(BF16)


def _conv_sample(state, u, w_dw, b_dw, ln_g, ln_b):
    db, d = u.shape
    args = (state, u, w_dw, b_dw, ln_g, ln_b)
    return pl.pallas_call(
        _conv_sample_kernel,
        grid=(1,),
        in_specs=[_resident(a.shape) for a in args],
        out_specs=pl.BlockSpec((db, d), lambda i: (0, 0)),
        out_shape=jax.ShapeDtypeStruct((db, d), BF16),
        scratch_shapes=[pltpu.VMEM((db, d), F32)],
        compiler_params=_compiler_params(("arbitrary",)),
        name="conv_sample",
    )(*args)


def kernel(x_prompt, x_sample, p_prompt, p_sample, cache_k, cache_v, cache_kidx, state_conv, page_table, ffn1_norm, ffn1_w_gate, ffn1_w_up, ffn1_w_down, mix_norm, attn_w_in, attn_q_gain, attn_k_gain, attn_kidx_gain, attn_w_o, conv_w_in, conv_b_in, conv_w_dw, conv_b_dw, conv_ln_g, conv_ln_b, conv_w_out, conv_b_out, ffn2_norm, ffn2_w_gate, ffn2_w_up, ffn2_w_down, ple_norm, ple_w_gate, ple_w_proj):
    batch, seq, d = x_prompt.shape
    db, t_new, _ = x_sample.shape
    depth = ffn1_norm.shape[0]
    n_mixers = 2
    n_pages = page_table.shape[1]
    past = n_pages * PAGE_SIZE
    attn_dim = N_HEADS * HEAD_DIM
    assert t_new == 1 and d == attn_dim
    topk_p = min(TOPK_MAX, seq // 4)
    topk_s = min(TOPK_MAX, (past + t_new) // 4)
    n_p = batch * seq
    tile_p = min(ROW_TILE, seq)
    tile_s = db

    row = lambda v: v.reshape(1, -1)
    bf = lambda w: w.astype(BF16)
    hp = x_prompt.reshape(n_p, d)
    hs = x_sample.reshape(db, d)
    cache_k4 = cache_k.reshape(cache_k.shape[0], cache_k.shape[1], PAGE_SIZE * N_HEADS, HEAD_DIM)
    cache_v4 = cache_v.reshape(cache_v.shape[0], cache_v.shape[1], PAGE_SIZE * N_HEADS, HEAD_DIM)
    zero_bias = jnp.zeros((1, d), F32)

    k_p, v_p, ki_p, conv_p = [], [], [], []
    k_s, v_s, ki_s, conv_s = [], [], [], []
    for i in range(depth):
        ffn1 = (row(ffn1_norm[i]), bf(ffn1_w_gate[i]), bf(ffn1_w_up[i]), bf(ffn1_w_down[i]))
        ffn2 = (row(ffn2_norm[i]), bf(ffn2_w_gate[i]), bf(ffn2_w_up[i]), bf(ffn2_w_down[i]))
        ple = (row(ple_norm[i]), bf(ple_w_gate[i]), bf(ple_w_proj[i]))
        g_mix = row(mix_norm[i])
        if i % n_mixers == 0:
            a = i // n_mixers
            w_in = attn_w_in[a]
            o1, o2, o3 = attn_dim, 2 * attn_dim, 3 * attn_dim
            o4 = o3 + IDX_HEADS * IDX_DIM
            o5 = o4 + IDX_DIM
            w_wi = jnp.pad(w_in[:, o5:], ((0, 0), (0, LANES - IDX_HEADS)))
            w_v = bf(w_in[:, o2:o3])
            proj = (bf(w_in[:, :o1]), bf(w_in[:, o1:o2]), w_v, w_v.T, bf(w_in[:, o3:o4]), bf(w_in[:, o4:o5]),
                    bf(w_wi), row(attn_q_gain[a]), row(attn_k_gain[a]), row(attn_kidx_gain[a]))
            w_out, b_out = bf(attn_w_o[a]), zero_bias

            hp, qb, kf, vf, kif, kb, vt, qib, kib, wi = _stage_a_attn(hp, ffn1, g_mix, proj, tile_p, batch)
            mp = _attn_prompt(qb, qib, wi, kb, vt, kib, batch, seq, topk_p)
            k_p.append(kf.reshape(batch, seq, N_HEADS, HEAD_DIM))
            v_p.append(vf.reshape(batch, seq, N_HEADS, HEAD_DIM))
            ki_p.append(kif.reshape(batch, seq, IDX_DIM))

            hs, qb, kf, vf, kif, kb, vt, qib, kib, wi = _stage_a_attn(hs, ffn1, g_mix, proj, tile_s, 1)
            qi8 = qib.reshape(db, IDX_HEADS, IDX_DIM)
            w8 = jnp.broadcast_to(wi[:, :IDX_HEADS, None], (db, IDX_HEADS, LANES))
            scores, score_self = _sample_scores(qi8, w8, kif.reshape(db, 1, IDX_DIM), cache_kidx, page_table, a)
            sel, sel_self = _sample_select(scores, score_self, topk_s)
            per_head = lambda v: v.astype(F32).reshape(db, N_HEADS, HEAD_DIM)
            ms = _sample_attn(per_head(qb), sel, sel_self, per_head(kf), per_head(vf), cache_k4, cache_v4,
                              page_table, a)
            ms = ms.reshape(db, d).astype(BF16)
            k_s.append(kf.reshape(db, t_new, N_HEADS, HEAD_DIM))
            v_s.append(vf.reshape(db, t_new, N_HEADS, HEAD_DIM))
            ki_s.append(kif.reshape(db, t_new, IDX_DIM))
        else:
            c = i // n_mixers
            w_dw = jnp.pad(conv_w_dw[c], ((0, CONV_HALO - CONV_WIDTH), (0, 0)))
            conv = (w_dw, row(conv_b_dw[c]), row(conv_ln_g[c]), row(conv_ln_b[c]))
            w_out, b_out = bf(conv_w_out[c]), row(conv_b_out[c])

            hp, glu = _stage_a_conv(hp, ffn1, g_mix, bf(conv_w_in[c]), row(conv_b_in[c]), tile_p)
            mp = _conv_prompt(glu, *conv, batch, seq)
            conv_p.append(glu.reshape(batch, seq, d)[:, seq - CONV_CTX:])

            hs, glu = _stage_a_conv(hs, ffn1, g_mix, bf(conv_w_in[c]), row(conv_b_in[c]), tile_s)
            ms = _conv_sample(state_conv[c], glu, *conv)
            conv_s.append(jnp.concatenate([state_conv[c], glu[:, None, :]], axis=1)[:, -CONV_CTX:])

        hp = _stage_c(hp, mp, p_prompt[i].reshape(n_p, -1), w_out, b_out, ffn2, ple, tile_p)
        hs = _stage_c(hs, ms, p_sample[i].reshape(db, -1), w_out, b_out, ffn2, ple, tile_s)

    return (hp.reshape(batch, seq, d), hs.reshape(db, t_new, d), jnp.stack(k_p), jnp.stack(v_p), jnp.stack(ki_p),
            jnp.stack(conv_p), jnp.stack(k_s), jnp.stack(v_s), jnp.stack(ki_s), jnp.stack(conv_s))
```

```python
import functools

import jax
import jax.numpy as jnp
from jax import lax
from jax.experimental import pallas as pl
from jax.experimental.pallas import tpu as pltpu

F32 = jnp.float32
BF16 = jnp.bfloat16

N_HEADS = 8
HEAD_DIM = 128
IDX_HEADS = 8
IDX_DIM = 128
TOPK_MAX = 256
PAGE_SIZE = 128
CONV_WIDTH = 31
CONV_CTX = CONV_WIDTH - 1
EPS = 1e-6

LANES = 128
SUBLANES = 8
VMEM_LIMIT_BYTES = 56 * 1024 * 1024
ROW_TILE = 512
FF_CHUNK = 256
ATTN_Q_TILE = 256
ATTN_K_CHUNK = 512
CONV_T_TILE = 256
CONV_HALO = 32
CONV_ROW_BLOCK = 32
N_BISECT = 15
FOLD_ROWS = 32
LOG2_E = 1.4426950408889634
V_ROWS = HEAD_DIM + 16
BISECT_STEPS_PER_CHECK = 3
IDX_PAGES_PER_STEP = 16
KV_PAGES_PER_STEP = 8

_NT = (((1,), (1,)), ((), ()))


def _compiler_params(semantics):
    return pltpu.CompilerParams(dimension_semantics=semantics, vmem_limit_bytes=VMEM_LIMIT_BYTES)


def _resident(shape):
    nd = len(shape)
    return pl.BlockSpec(shape, lambda *_: (0,) * nd, pipeline_mode=pl.Buffered(1))


def _rows(tile, cols):
    return pl.BlockSpec((tile, cols), lambda i: (i, 0))


def _rms(x, g):
    return x * lax.rsqrt(jnp.mean(x * x, axis=-1, keepdims=True) + EPS) * g


def _dot(a, b):
    return jnp.dot(a, b, preferred_element_type=F32)


def _dot_nt(a, b):
    return lax.dot_general(a, b, _NT, preferred_element_type=F32)


def _ffn_half_step(x, g_ref, wg_ref, wu_ref, wd_ref, act_ref):
    hn = _rms(x, g_ref[...]).astype(BF16)
    d_ff = wg_ref.shape[1]
    for c in range(d_ff // FF_CHUNK):
        sl = slice(c * FF_CHUNK, (c + 1) * FF_CHUNK)
        g = _dot(hn, wg_ref[:, sl])
        u = _dot(hn, wu_ref[:, sl])
        act_ref[:, sl] = (g * jax.nn.sigmoid(g) * u).astype(BF16)
    return x + 0.5 * _dot(act_ref[...], wd_ref[...])


def _stage_a_attn_kernel(x_ref, g1_ref, wg_ref, wu_ref, wd_ref, gm_ref, wq_ref, wk_ref, wv_ref, wvt_ref, wqi_ref,
                         wki_ref, wwi_ref, qg_ref, kg_ref, kig_ref,
                         h_ref, qb_ref, kf_ref, vf_ref, kif_ref, kb_ref, vt_ref, qib_ref, kib_ref, wi_ref,
                         act_ref):
    h = _ffn_half_step(x_ref[...], g1_ref, wg_ref, wu_ref, wd_ref, act_ref)
    h_ref[...] = h
    hn = _rms(h, gm_ref[...]).astype(BF16)

    zq = _dot(hn, wq_ref[...])
    for hh in range(N_HEADS):
        sl = slice(hh * HEAD_DIM, (hh + 1) * HEAD_DIM)
        qb_ref[:, sl] = (_rms(zq[:, sl], qg_ref[...]) * (HEAD_DIM ** -0.5 * LOG2_E)).astype(BF16)
    zk = _dot(hn, wk_ref[...])
    for hh in range(N_HEADS):
        sl = slice(hh * HEAD_DIM, (hh + 1) * HEAD_DIM)
        kn = _rms(zk[:, sl], kg_ref[...])
        kf_ref[:, sl] = kn
        kb_ref[:, sl] = kn.astype(BF16)
    vf_ref[...] = _dot(hn, wv_ref[...])
    vt = _dot_nt(wvt_ref[...], hn).astype(BF16)
    for hh in range(N_HEADS):
        vt_ref[hh * V_ROWS:hh * V_ROWS + HEAD_DIM, :] = vt[hh * HEAD_DIM:(hh + 1) * HEAD_DIM, :]
        vt_ref[hh * V_ROWS + HEAD_DIM:(hh + 1) * V_ROWS, :] = jnp.ones((V_ROWS - HEAD_DIM, vt.shape[1]), BF16)
    qib_ref[...] = _dot(hn, wqi_ref[...]).astype(BF16)
    kin = _rms(_dot(hn, wki_ref[...]), kig_ref[...])
    kif_ref[...] = kin
    kib_ref[...] = kin.astype(BF16)
    wi_ref[...] = _dot(hn, wwi_ref[...])


def _stage_a_attn(x, ffn, g_mix, proj, tile, batch):
    n, d = x.shape
    g1, wg, wu, wd = ffn
    wq, wk, wv, wvt, wqi, wki, wwi, qg, kg, kig = proj
    d_ff = wg.shape[1]
    consts = (g1, wg, wu, wd, g_mix, wq, wk, wv, wvt, wqi, wki, wwi, qg, kg, kig)
    sds = jax.ShapeDtypeStruct
    seq = n // batch
    tiles_per_seq = seq // tile
    out_shape = (sds((n, d), F32), sds((n, d), BF16), sds((n, d), F32), sds((n, d), F32),
                 sds((n, IDX_DIM), F32), sds((n, d), BF16), sds((batch, N_HEADS * V_ROWS, seq), BF16), sds((n, d), BF16),
                 sds((n, IDX_DIM), BF16), sds((n, LANES), F32))
    out_specs = [_rows(tile, s.shape[1]) for s in out_shape]
    out_specs[6] = pl.BlockSpec((None, N_HEADS * V_ROWS, tile),
                                lambda i: (i // tiles_per_seq, 0, i % tiles_per_seq))
    return pl.pallas_call(
        _stage_a_attn_kernel,
        grid=(n // tile,),
        in_specs=[_rows(tile, d)] + [_resident(c.shape) for c in consts],
        out_specs=out_specs,
        out_shape=out_shape,
        scratch_shapes=[pltpu.VMEM((tile, d_ff), BF16)],
        compiler_params=_compiler_params(("parallel",)),
        name="stage_a_attn",
    )(x, *consts)


def _stage_a_conv_kernel(x_ref, g1_ref, wg_ref, wu_ref, wd_ref, gm_ref, wci_ref, bci_ref,
                         h_ref, glu_ref, act_ref):
    h = _ffn_half_step(x_ref[...], g1_ref, wg_ref, wu_ref, wd_ref, act_ref)
    h_ref[...] = h
    hn = _rms(h, gm_ref[...]).astype(BF16)
    d = h.shape[1]
    u = _dot(hn, wci_ref[...]) + bci_ref[...]
    glu_ref[...] = u[:, :d] * jax.nn.sigmoid(u[:, d:])


def _stage_a_conv(x, ffn, g_mix, wci, bci, tile):
    n, d = x.shape
    g1, wg, wu, wd = ffn
    consts = (g1, wg, wu, wd, g_mix, wci, bci)
    out_shape = (jax.ShapeDtypeStruct((n, d), F32), jax.ShapeDtypeStruct((n, d), F32))
    return pl.pallas_call(
        _stage_a_conv_kernel,
        grid=(n // tile,),
        in_specs=[_rows(tile, d)] + [_resident(c.shape) for c in consts],
        out_specs=[_rows(tile, d), _rows(tile, d)],
        out_shape=out_shape,
        scratch_shapes=[pltpu.VMEM((tile, wg.shape[1]), BF16)],
        compiler_params=_compiler_params(("parallel",)),
        name="stage_a_conv",
    )(x, *consts)


def _stage_c_kernel(h_ref, m_ref, p_ref, wo_ref, bo_ref, g2_ref, wg_ref, wu_ref, wd_ref, gp_ref, wpg_ref,
                    wpp_ref, out_ref, act_ref):
    h = h_ref[...] + _dot(m_ref[...], wo_ref[...]) + bo_ref[...]
    h = _ffn_half_step(h, g2_ref, wg_ref, wu_ref, wd_ref, act_ref)
    gate = jax.nn.sigmoid(_dot(_rms(h, gp_ref[...]).astype(BF16), wpg_ref[...]))
    out_ref[...] = h + gate * _dot(p_ref[...].astype(BF16), wpp_ref[...])


def _stage_c(h, m, p, wo, bo, ffn, ple, tile):
    n, d = h.shape
    g2, wg, wu, wd = ffn
    gp, wpg, wpp = ple
    consts = (wo, bo, g2, wg, wu, wd, gp, wpg, wpp)
    return pl.pallas_call(
        _stage_c_kernel,
        grid=(n // tile,),
        in_specs=[_rows(tile, d), _rows(tile, d), _rows(tile, p.shape[1])] + [_resident(c.shape) for c in consts],
        out_specs=_rows(tile, d),
        out_shape=jax.ShapeDtypeStruct((n, d), F32),
        scratch_shapes=[pltpu.VMEM((tile, wg.shape[1]), BF16)],
        compiler_params=_compiler_params(("parallel",)),
        name="stage_c",
    )(h, m, p, *consts)


def _any(flag):
    return jnp.max(jnp.where(flag, 1.0, 0.0)).astype(jnp.int32)


def _topk_bracket(count_ge, max_below, row_min, row_max, n_valid, kk):
    def update(lo, hi, clo, chi, p, c):
        pend = clo != kk
        up = pend & (c >= kk)
        dn = pend & (c < kk)
        return (jnp.where(up, p, lo), jnp.where(dn, p, hi), jnp.where(up, c, clo), jnp.where(dn, c, chi))

    state = update(row_min, jnp.full_like(row_min, jnp.inf), n_valid, jnp.zeros_like(row_min),
                   row_max, count_ge(row_max))

    def bisect_cond(carry):
        return (carry[0] < N_BISECT) & (carry[1] > 0)

    def bisect_body(carry):
        it, _, *st = carry
        for _ in range(BISECT_STEPS_PER_CHECK):
            p = 0.5 * st[0] + 0.5 * st[1]
            st = update(*st, p, count_ge(p))
        return (it + BISECT_STEPS_PER_CHECK, _any(st[2] != kk)) + tuple(st)

    carry = lax.while_loop(bisect_cond, bisect_body, (jnp.int32(0), _any(state[2] != kk)) + state)
    lo, hi, clo, chi = carry[2:]
    tied = jnp.zeros_like(lo)

    def peel_cond(carry):
        return carry[0] > 0

    def peel_body(carry):
        _, lo, hi, clo, chi, tied = carry
        m = max_below(hi)
        c = count_ge(m)
        pend = (clo != kk) & (tied == 0.0)
        fin = pend & (c >= kk)
        dn = pend & (c < kk)
        lo = jnp.where(fin, m, lo)
        clo = jnp.where(fin, c, clo)
        tied = jnp.where(fin, 1.0, tied)
        hi = jnp.where(dn, m, hi)
        chi = jnp.where(dn, c, chi)
        return (_any((clo != kk) & (tied == 0.0)), lo, hi, clo, chi, tied)

    carry = lax.while_loop(peel_cond, peel_body, (_any(clo != kk), lo, hi, clo, chi, tied))
    return carry[1:5]


def _fold_rows(x, op):
    return op(x.reshape(x.shape[0] // FOLD_ROWS, FOLD_ROWS, x.shape[1]), axis=0)


def _attn_prompt_kernel(q_ref, qi_ref, wi_ref, k_ref, vt_ref, ki_ref, o_ref, score_ref, wt_ref, acc_ref, s_ref, *,
                        topk):
    tq = q_ref.shape[0]
    seq = k_ref.shape[0]
    kc = min(ATTN_K_CHUNK, seq)
    q0 = pl.program_id(1) * tq
    n_ch = (q0 + tq + kc - 1) // kc
    q_pos = q0 + lax.broadcasted_iota(jnp.int32, (1, tq), 1)

    def chunk(c):
        return pl.ds(pl.multiple_of(c * kc, kc), kc)

    wt_ref[...] = wi_ref[...].T

    def score_body(c, carry):
        mx, mn = carry
        kic = ki_ref[chunk(c), :]
        acc = jnp.zeros((kc, tq), F32)
        for hh in range(IDX_HEADS):
            d = _dot_nt(kic, qi_ref[:, hh * IDX_DIM:(hh + 1) * IDX_DIM])
            acc = acc + wt_ref[hh:hh + 1, :] * jnp.maximum(d, 0.0)
        key_pos = c * kc + lax.broadcasted_iota(jnp.int32, (kc, 1), 0)
        allowed = key_pos <= q_pos
        sc = jnp.where(allowed, acc, -jnp.inf)
        score_ref[chunk(c), :] = sc
        mx = jnp.maximum(mx, _fold_rows(sc, jnp.max))
        mn = jnp.minimum(mn, _fold_rows(jnp.where(allowed, acc, jnp.inf), jnp.min))
        return mx, mn

    mx, mn = lax.fori_loop(0, n_ch, score_body,
                           (jnp.full((FOLD_ROWS, tq), -jnp.inf, F32), jnp.full((FOLD_ROWS, tq), jnp.inf, F32)))
    row_max = jnp.max(mx, axis=0, keepdims=True)
    row_min = jnp.min(mn, axis=0, keepdims=True)

    def count_ge(p):
        def body(c, acc):
            return acc + _fold_rows(jnp.where(score_ref[chunk(c), :] >= p, 1.0, 0.0), jnp.sum)
        acc = lax.fori_loop(0, n_ch, body, jnp.zeros((FOLD_ROWS, tq), F32))
        return jnp.sum(acc, axis=0, keepdims=True)

    def max_below(hi):
        def body(c, acc):
            x = score_ref[chunk(c), :]
            return jnp.maximum(acc, _fold_rows(jnp.where(x < hi, x, -jnp.inf), jnp.max))
        acc = lax.fori_loop(0, n_ch, body, jnp.full((FOLD_ROWS, tq), -jnp.inf, F32))
        return jnp.max(acc, axis=0, keepdims=True)

    n_valid = (q_pos + 1).astype(F32)
    kk = jnp.minimum(n_valid, float(topk))
    lo, hi, clo, chi = _topk_bracket(count_ge, max_below, row_min, row_max, n_valid, kk)
    has_ties = _any(clo != kk) > 0

    @pl.when(jnp.logical_not(has_ties))
    def _():
        def body(c, carry):
            score_ref[chunk(c), :] = jnp.where(score_ref[chunk(c), :] >= lo, 0.0, -jnp.inf)
            return carry
        lax.fori_loop(0, n_ch, body, 0)

    @pl.when(has_ties)
    def _():
        need = kk - chi
        upto = jnp.where(lax.broadcasted_iota(jnp.int32, (kc, kc), 1) <= lax.broadcasted_iota(jnp.int32, (kc, kc), 0),
                         1.0, 0.0).astype(BF16)

        def body(c, before):
            x = score_ref[chunk(c), :]
            cand = jnp.where((x >= lo) & (x < hi), 1.0, 0.0)
            rank = before + _dot(upto, cand.astype(BF16))
            take = (cand > 0.0) & (rank <= need)
            score_ref[chunk(c), :] = jnp.where((x >= hi) | take, 0.0, -jnp.inf)
            return before + jnp.sum(cand, axis=0, keepdims=True)
        lax.fori_loop(0, n_ch, body, jnp.zeros((1, tq), F32))

    acc_ref[...] = jnp.zeros(acc_ref.shape, F32)

    def logits(c, slot, m_old):
        bias = score_ref[chunk(c), :]
        m_loc = []
        for hh in range(N_HEADS):
            hs = slice(hh * HEAD_DIM, (hh + 1) * HEAD_DIM)
            s = _dot_nt(k_ref[chunk(c), hs], q_ref[:, hs]) + bias
            s_ref[slot, hh] = s
            m_loc.append(jnp.max(s, axis=0, keepdims=True))
        return jnp.maximum(m_old, jnp.concatenate(m_loc, axis=0))

    def softmax_pv(c, slot, m_acc, m_cur):
        m_safe = jnp.where(m_cur == -jnp.inf, 0.0, m_cur)
        alpha = jnp.exp2(m_acc - m_safe)
        for hh in range(N_HEADS):
            vs = slice(hh * V_ROWS, (hh + 1) * V_ROWS)
            p = jnp.exp2(s_ref[slot, hh] - m_safe[hh:hh + 1, :])
            acc_ref[vs, :] = alpha[hh:hh + 1, :] * acc_ref[vs, :] + _dot(vt_ref[vs, chunk(c)], p.astype(BF16))

    def pair_body(i, carry):
        m_acc, m_cur = carry
        m_1 = logits(2 * i + 1, 1, m_cur)
        softmax_pv(2 * i, 0, m_acc, m_cur)
        m_2 = logits(jnp.minimum(2 * i + 2, n_ch - 1), 0, m_1)
        softmax_pv(2 * i + 1, 1, m_cur, m_1)
        return m_1, m_2

    no_max = jnp.full((N_HEADS, tq), -jnp.inf, F32)
    m_acc, m_cur = lax.fori_loop(0, n_ch // 2, pair_body, (no_max, logits(0, 0, no_max)))

    @pl.when(n_ch % 2 == 1)
    def _():
        softmax_pv(n_ch - 1, 0, m_acc, m_cur)
    for hh in range(N_HEADS):
        r0 = hh * V_ROWS
        out = acc_ref[r0:r0 + HEAD_DIM, :] / acc_ref[r0 + HEAD_DIM:r0 + HEAD_DIM + 1, :]
        o_ref[:, hh * HEAD_DIM:(hh + 1) * HEAD_DIM] = out.T.astype(BF16)


def _attn_prompt(qb, qib, wi, kb, vt, kib, batch, seq, topk):
    n, d = qb.shape
    tq = min(ATTN_Q_TILE, seq)
    kc = min(ATTN_K_CHUNK, seq)
    nq = seq // tq
    tile_map = lambda b, j: (b * nq + j, 0)
    batch_map = lambda b, j: (b, 0)
    return pl.pallas_call(
        functools.partial(_attn_prompt_kernel, topk=topk),
        grid=(batch, nq),
        in_specs=[pl.BlockSpec((tq, d), tile_map), pl.BlockSpec((tq, d), tile_map),
                  pl.BlockSpec((tq, LANES), tile_map),
                  pl.BlockSpec((seq, d), batch_map, pipeline_mode=pl.Buffered(1)),
                  pl.BlockSpec((None, N_HEADS * V_ROWS, seq), lambda b, j: (b, 0, 0), pipeline_mode=pl.Buffered(1)),
                  pl.BlockSpec((seq, IDX_DIM), batch_map, pipeline_mode=pl.Buffered(1))],
        out_specs=pl.BlockSpec((tq, d), tile_map),
        out_shape=jax.ShapeDtypeStruct((n, d), BF16),
        scratch_shapes=[pltpu.VMEM((seq, tq), F32), pltpu.VMEM((LANES, tq), F32),
                        pltpu.VMEM((N_HEADS * V_ROWS, tq), F32), pltpu.VMEM((2, N_HEADS, kc, tq), F32)],
        compiler_params=_compiler_params(("parallel", "arbitrary")),
        name="attn_prompt",
    )(qb, qib, wi, kb, vt, kib)


def _sample_scores_kernel(pt_ref, qi_ref, w_ref, kin_ref, *rest, pages_per_step):
    del pt_ref
    page_refs = rest[:pages_per_step]
    score_ref, self_ref = rest[pages_per_step:]
    qi = qi_ref[...]
    w = w_ref[...]

    rows = []
    for r in range(pages_per_step):
        d = _dot_nt(qi, page_refs[r][...].astype(BF16))
        rows.append(jnp.sum(w * jnp.maximum(d, 0.0), axis=0, keepdims=True))
    score_ref[...] = jnp.concatenate(rows, axis=0)

    @pl.when(pl.program_id(1) == 0)
    def _():
        d_self = jnp.sum(qi.astype(F32) * kin_ref[...].astype(BF16).astype(F32), axis=1, keepdims=True)
        self_ref[...] = jnp.sum(w * jnp.maximum(d_self, 0.0), axis=0, keepdims=True)


def _sample_scores(qi8, w8, ki_new, cache_kidx, page_table, layer):
    db, n_pages = page_table.shape
    pps = min(IDX_PAGES_PER_STEP, n_pages)
    steps = n_pages // pps

    def page_spec(r):
        return pl.BlockSpec((None, None, PAGE_SIZE, IDX_DIM), lambda b, g, pt: (layer, pt[b, g * pps + r], 0, 0))

    per_seq = lambda b, g, pt: (b, 0, 0)
    grid_spec = pltpu.PrefetchScalarGridSpec(
        num_scalar_prefetch=1,
        grid=(db, steps),
        in_specs=[pl.BlockSpec((None, IDX_HEADS, IDX_DIM), per_seq), pl.BlockSpec((None, IDX_HEADS, LANES), per_seq),
                  pl.BlockSpec((None, 1, IDX_DIM), per_seq)] + [page_spec(r) for r in range(pps)],
        out_specs=[pl.BlockSpec((None, pps, PAGE_SIZE), lambda b, g, pt: (b, g, 0)),
                   pl.BlockSpec((None, 1, LANES), per_seq)],
    )
    return pl.pallas_call(
        functools.partial(_sample_scores_kernel, pages_per_step=pps),
        grid_spec=grid_spec,
        out_shape=(jax.ShapeDtypeStruct((db, n_pages, PAGE_SIZE), F32), jax.ShapeDtypeStruct((db, 1, LANES), F32)),
        compiler_params=_compiler_params(("parallel", "arbitrary")),
        name="sample_scores",
    )(page_table, qi8, w8, ki_new, *([cache_kidx] * pps))


def _sample_select_kernel(x_ref, xself_ref, sel_ref, selself_ref, *, topk):
    x = x_ref[...]
    x_self = xself_ref[...][:, :, :1]
    db, n_pages, ps = x.shape

    def total(v, op):
        return op(op(v, axis=1, keepdims=True), axis=2, keepdims=True)

    def count_ge(p):
        return total(jnp.where(x >= p, 1.0, 0.0), jnp.sum) + jnp.where(x_self >= p, 1.0, 0.0)

    def max_below(hi):
        return jnp.maximum(total(jnp.where(x < hi, x, -jnp.inf), jnp.max), jnp.where(x_self < hi, x_self, -jnp.inf))

    row_max = jnp.maximum(total(x, jnp.max), x_self)
    row_min = jnp.minimum(total(x, jnp.min), x_self)
    n_valid = jnp.full((db, 1, 1), float(n_pages * ps + 1), F32)
    kk = jnp.full((db, 1, 1), float(topk), F32)
    lo, hi, clo, chi = _topk_bracket(count_ge, max_below, row_min, row_max, n_valid, kk)
    has_ties = _any(clo != kk) > 0

    @pl.when(jnp.logical_not(has_ties))
    def _():
        sel_ref[...] = jnp.where(x >= lo, 1.0, 0.0)
        selself_ref[...] = jnp.broadcast_to(jnp.where(x_self >= lo, 1.0, 0.0), selself_ref.shape)

    @pl.when(has_ties)
    def _():
        need = kk - chi
        cand = jnp.where((x >= lo) & (x < hi), 1.0, 0.0)
        upto = jnp.where(lax.broadcasted_iota(jnp.int32, (ps, ps), 0) <= lax.broadcasted_iota(jnp.int32, (ps, ps), 1),
                         1.0, 0.0).astype(BF16)
        below = jnp.where(lax.broadcasted_iota(jnp.int32, (n_pages, n_pages), 1)
                          < lax.broadcasted_iota(jnp.int32, (n_pages, n_pages), 0), 1.0, 0.0).astype(BF16)
        in_page = _dot(cand.reshape(db * n_pages, ps).astype(BF16), upto).reshape(db, n_pages, ps)
        page_tot = jnp.broadcast_to(jnp.sum(cand, axis=2, keepdims=True), cand.shape).astype(BF16)
        for b in range(db):
            rank = in_page[b] + _dot(below, page_tot[b])
            take = (cand[b] > 0.0) & (rank <= need[b])
            sel_ref[b] = jnp.where((x[b] >= hi[b]) | take, 1.0, 0.0)
        cand_self = (x_self >= lo) & (x_self < hi)
        take_self = cand_self & (total(cand, jnp.sum) + 1.0 <= need)
        selself_ref[...] = jnp.broadcast_to(jnp.where((x_self >= hi) | take_self, 1.0, 0.0), selself_ref.shape)


def _sample_select(scores, score_self, topk):
    args = (scores, score_self)
    return pl.pallas_call(
        functools.partial(_sample_select_kernel, topk=topk),
        grid=(1,),
        in_specs=[_resident(a.shape) for a in args],
        out_specs=[pl.BlockSpec(a.shape, lambda i: (0, 0, 0)) for a in args],
        out_shape=tuple(jax.ShapeDtypeStruct(a.shape, F32) for a in args),
        compiler_params=_compiler_params(("arbitrary",)),
        name="sample_select",
    )(*args)


def _sample_attn_kernel(pt_ref, q_ref, sel_ref, selself_ref, kn_ref, vn_ref, spread_ref, *rest, pages_per_step):
    del pt_ref
    k_refs = rest[:pages_per_step]
    v_refs = rest[pages_per_step:2 * pages_per_step]
    o_ref, m_ref, l_ref, acc_ref = rest[2 * pages_per_step:]
    g = pl.program_id(1)
    q = q_ref[...]
    q_bf = q.astype(BF16)
    cols = PAGE_SIZE * N_HEADS
    own = (lax.broadcasted_iota(jnp.int32, (N_HEADS, cols), 1) % N_HEADS
           == lax.broadcasted_iota(jnp.int32, (N_HEADS, cols), 0))

    @pl.when(g == 0)
    def _():
        m_ref[...] = jnp.full(m_ref.shape, -jnp.inf, F32)
        l_ref[...] = jnp.zeros(l_ref.shape, F32)
        acc_ref[...] = jnp.zeros(acc_ref.shape, F32)

    def accumulate(logits, value_fns):
        m_old = m_ref[...]
        m_new = m_old
        for s in logits:
            m_new = jnp.maximum(m_new, jnp.max(s, axis=1, keepdims=True))
        m_safe = jnp.where(m_new == -jnp.inf, 0.0, m_new)
        alpha = jnp.exp2(m_old - m_safe)
        l = alpha * l_ref[...]
        acc = alpha * acc_ref[...]
        for s, value_fn in zip(logits, value_fns):
            p = jnp.exp2(s - m_safe)
            l = l + jnp.sum(p, axis=1, keepdims=True)
            acc = acc + value_fn(p)
        l_ref[...] = l
        acc_ref[...] = acc
        m_ref[...] = m_new

    sel_cols = _dot(sel_ref[...].astype(BF16), spread_ref[...])
    logits = []
    for r in range(pages_per_step):
        s = _dot_nt(q_bf, k_refs[r][...].astype(BF16))
        logits.append(jnp.where(own & (sel_cols[r:r + 1, :] > 0.0), s, -jnp.inf))
    accumulate(logits, [lambda p, r=r: _dot(p.astype(BF16), v_refs[r][...].astype(BF16))
                        for r in range(pages_per_step)])

    @pl.when(g == pl.num_programs(1) - 1)
    def _():
        kn = kn_ref[...].astype(BF16).astype(F32)
        s_self = jnp.sum(q * kn, axis=1, keepdims=True)
        s_self = jnp.where(selself_ref[:, :1] > 0.0, s_self, -jnp.inf)
        vn = vn_ref[...].astype(BF16).astype(F32)
        accumulate([s_self], [lambda p: p.astype(BF16).astype(F32) * vn])
        o_ref[...] = acc_ref[...] / l_ref[...]


def _sample_attn(q, sel, sel_self, k_new, v_new, cache_k, cache_v, page_table, layer):
    db, n_pages = page_table.shape
    pps = min(KV_PAGES_PER_STEP, n_pages)
    steps = n_pages // pps
    cols = PAGE_SIZE * N_HEADS
    sel4 = sel.reshape(db, steps, pps, PAGE_SIZE)
    spread = (jnp.arange(cols)[None, :] // N_HEADS == jnp.arange(PAGE_SIZE)[:, None]).astype(BF16)

    def page_spec(r):
        return pl.BlockSpec((None, None, cols, HEAD_DIM), lambda b, g, pt: (layer, pt[b, g * pps + r], 0, 0))

    per_seq = lambda b, g, pt: (b, 0, 0)
    heads_spec = pl.BlockSpec((None, N_HEADS, HEAD_DIM), per_seq)
    grid_spec = pltpu.PrefetchScalarGridSpec(
        num_scalar_prefetch=1,
        grid=(db, steps),
        in_specs=[heads_spec,
                  pl.BlockSpec((None, None, pps, PAGE_SIZE), lambda b, g, pt: (b, g, 0, 0)),
                  pl.BlockSpec((None, 1, LANES), per_seq), heads_spec, heads_spec,
                  pl.BlockSpec((PAGE_SIZE, cols), lambda b, g, pt: (0, 0))]
                 + [page_spec(r) for r in range(pps)] * 2,
        out_specs=heads_spec,
        scratch_shapes=[pltpu.VMEM((N_HEADS, 1), F32), pltpu.VMEM((N_HEADS, 1), F32),
                        pltpu.VMEM((N_HEADS, HEAD_DIM), F32)],
    )
    return pl.pallas_call(
        functools.partial(_sample_attn_kernel, pages_per_step=pps),
        grid_spec=grid_spec,
        out_shape=jax.ShapeDtypeStruct((db, N_HEADS, HEAD_DIM), F32),
        compiler_params=_compiler_params(("parallel", "arbitrary")),
        name="sample_attn",
    )(page_table, q, sel4, sel_self, k_new, v_new, spread, *([cache_k] * pps), *([cache_v] * pps))


def _ln_swish(y, g, b):
    yc = y - jnp.mean(y, axis=-1, keepdims=True)
    yn = yc * lax.rsqrt(jnp.mean(yc * yc, axis=-1, keepdims=True) + EPS) * g + b
    return yn * jax.nn.sigmoid(yn)


def _conv_prompt_kernel(cur_ref, halo_ref, w_ref, bdw_ref, lng_ref, lnb_ref, y_ref, sh_ref, conv_ref):
    tt, d = cur_ref.shape
    sh_ref[0, :CONV_HALO, :] = jnp.where(pl.program_id(1) == 0, 0.0, halo_ref[...])
    sh_ref[0, CONV_HALO:, :] = cur_ref[...]
    n_sh = tt + CONV_HALO - SUBLANES
    for r in range(1, SUBLANES):
        sh_ref[r, :n_sh, :] = sh_ref[0, r:r + n_sh, :]
    first = CONV_HALO - CONV_CTX

    def block(rb, carry):
        r0 = pl.multiple_of(rb * CONV_ROW_BLOCK, CONV_ROW_BLOCK)
        for lt in range(d // LANES):
            ls = slice(lt * LANES, (lt + 1) * LANES)
            acc = jnp.zeros((CONV_ROW_BLOCK, LANES), F32)
            for r in range(SUBLANES):
                taps = [j for j in range(CONV_WIDTH) if (j + first) % SUBLANES == r]
                a_max = max((j + first) // SUBLANES for j in taps)
                rows = sh_ref[r, pl.ds(r0, a_max * SUBLANES + CONV_ROW_BLOCK), ls]
                for j in taps:
                    a = (j + first) // SUBLANES
                    acc = acc + w_ref[j:j + 1, ls] * rows[a * SUBLANES:a * SUBLANES + CONV_ROW_BLOCK]
            conv_ref[pl.ds(r0, CONV_ROW_BLOCK), ls] = acc
        return carry

    lax.fori_loop(0, tt // CONV_ROW_BLOCK, block, 0)
    y = conv_ref[...] + bdw_ref[...]
    y_ref[...] = _ln_swish(y, lng_ref[...], lnb_ref[...]).astype(BF16)


def _conv_prompt(glu, w_dw, b_dw, ln_g, ln_b, batch, seq):
    n, d = glu.shape
    tt = min(CONV_T_TILE, seq)
    nt = seq // tt
    halo_per_tile = tt // CONV_HALO
    halo_per_seq = seq // CONV_HALO
    tile_map = lambda b, j: (b * nt + j, 0)
    halo_map = lambda b, j: (jnp.maximum(b * halo_per_seq + j * halo_per_tile - 1, 0), 0)
    consts = (w_dw, b_dw, ln_g, ln_b)
    return pl.pallas_call(
        _conv_prompt_kernel,
        grid=(batch, nt),
        in_specs=[pl.BlockSpec((tt, d), tile_map), pl.BlockSpec((CONV_HALO, d), halo_map)]
                 + [_resident(c.shape) for c in consts],
        out_specs=pl.BlockSpec((tt, d), tile_map),
        out_shape=jax.ShapeDtypeStruct((n, d), BF16),
        scratch_shapes=[pltpu.VMEM((SUBLANES, tt + CONV_HALO, d), F32), pltpu.VMEM((tt, d), F32)],
        compiler_params=_compiler_params(("parallel", "arbitrary")),
        name="conv_prompt",
    )(glu, glu, *consts)


def _conv_sample_kernel(state_ref, u_ref, w_ref, bdw_ref, lng_ref, lnb_ref, y_ref, conv_ref):
    db = u_ref.shape[0]
    w_ctx = w_ref[:CONV_CTX, :]
    for b in range(db):
        conv_ref[b:b + 1, :] = jnp.sum(state_ref[b] * w_ctx, axis=0, keepdims=True)
    y = conv_ref[...] + u_ref[...] * w_ref[CONV_CTX:CONV_CTX + 1, :] + bdw_ref[...]
    y_ref[...] = _ln_swish(y, lng_ref[...], lnb_ref[...]).astype(BF16)


def _conv_sample(state, u, w_dw, b_dw, ln_g, ln_b):
    db, d = u.shape
    args = (state, u, w_dw, b_dw, ln_g, ln_b)
    return pl.pallas_call(
        _conv_sample_kernel,
        grid=(1,),
        in_specs=[_resident(a.shape) for a in args],
        out_specs=pl.BlockSpec((db, d), lambda i: (0, 0)),
        out_shape=jax.ShapeDtypeStruct((db, d), BF16),
        scratch_shapes=[pltpu.VMEM((db, d), F32)],
        compiler_params=_compiler_params(("arbitrary",)),
        name="conv_sample",
    )(*args)


def kernel(x_prompt, x_sample, p_prompt, p_sample, cache_k, cache_v, cache_kidx, state_conv, page_table, ffn1_norm, ffn1_w_gate, ffn1_w_up, ffn1_w_down, mix_norm, attn_w_in, attn_q_gain, attn_k_gain, attn_kidx_gain, attn_w_o, conv_w_in, conv_b_in, conv_w_dw, conv_b_dw, conv_ln_g, conv_ln_b, conv_w_out, conv_b_out, ffn2_norm, ffn2_w_gate, ffn2_w_up, ffn2_w_down, ple_norm, ple_w_gate, ple_w_proj):
    batch, seq, d = x_prompt.shape
    db, t_new, _ = x_sample.shape
    depth = ffn1_norm.shape[0]
    n_mixers = 2
    n_pages = page_table.shape[1]
    past = n_pages * PAGE_SIZE
    attn_dim = N_HEADS * HEAD_DIM
    assert t_new == 1 and d == attn_dim
    topk_p = min(TOPK_MAX, seq // 4)
    topk_s = min(TOPK_MAX, (past + t_new) // 4)
    n_p = batch * seq
    tile_p = min(ROW_TILE, seq)
    tile_s = db

    row = lambda v: v.reshape(1, -1)
    bf = lambda w: w.astype(BF16)
    hp = x_prompt.reshape(n_p, d)
    hs = x_sample.reshape(db, d)
    cache_k4 = cache_k.reshape(cache_k.shape[0], cache_k.shape[1], PAGE_SIZE * N_HEADS, HEAD_DIM)
    cache_v4 = cache_v.reshape(cache_v.shape[0], cache_v.shape[1], PAGE_SIZE * N_HEADS, HEAD_DIM)
    zero_bias = jnp.zeros((1, d), F32)

    k_p, v_p, ki_p, conv_p = [], [], [], []
    k_s, v_s, ki_s, conv_s = [], [], [], []
    for i in range(depth):
        ffn1 = (row(ffn1_norm[i]), bf(ffn1_w_gate[i]), bf(ffn1_w_up[i]), bf(ffn1_w_down[i]))
        ffn2 = (row(ffn2_norm[i]), bf(ffn2_w_gate[i]), bf(ffn2_w_up[i]), bf(ffn2_w_down[i]))
        ple = (row(ple_norm[i]), bf(ple_w_gate[i]), bf(ple_w_proj[i]))
        g_mix = row(mix_norm[i])
        if i % n_mixers == 0:
            a = i // n_mixers
            w_in = attn_w_in[a]
            o1, o2, o3 = attn_dim, 2 * attn_dim, 3 * attn_dim
            o4 = o3 + IDX_HEADS * IDX_DIM
            o5 = o4 + IDX_DIM
            w_wi = jnp.pad(w_in[:, o5:], ((0, 0), (0, LANES - IDX_HEADS)))
            w_v = bf(w_in[:, o2:o3])
            proj = (bf(w_in[:, :o1]), bf(w_in[:, o1:o2]), w_v, w_v.T, bf(w_in[:, o3:o4]), bf(w_in[:, o4:o5]),
                    bf(w_wi), row(attn_q_gain[a]), row(attn_k_gain[a]), row(attn_kidx_gain[a]))
            w_out, b_out = bf(attn_w_o[a]), zero_bias

            hp, qb, kf, vf, kif, kb, vt, qib, kib, wi = _stage_a_attn(hp, ffn1, g_mix, proj, tile_p, batch)
            mp = _attn_prompt(qb, qib, wi, kb, vt, kib, batch, seq, topk_p)
            k_p.append(kf.reshape(batch, seq, N_HEADS, HEAD_DIM))
            v_p.append(vf.reshape(batch, seq, N_HEADS, HEAD_DIM))
            ki_p.append(kif.reshape(batch, seq, IDX_DIM))

            hs, qb, kf, vf, kif, kb, vt, qib, kib, wi = _stage_a_attn(hs, ffn1, g_mix, proj, tile_s, 1)
            qi8 = qib.reshape(db, IDX_HEADS, IDX_DIM)
            w8 = jnp.broadcast_to(wi[:, :IDX_HEADS, None], (db, IDX_HEADS, LANES))
            scores, score_self = _sample_scores(qi8, w8, kif.reshape(db, 1, IDX_DIM), cache_kidx, page_table, a)
            sel, sel_self = _sample_select(scores, score_self, topk_s)
            per_head = lambda v: v.astype(F32).reshape(db, N_HEADS, HEAD_DIM)
            ms = _sample_attn(per_head(qb), sel, sel_self, per_head(kf), per_head(vf), cache_k4, cache_v4,
                              page_table, a)
            ms = ms.reshape(db, d).astype(BF16)
            k_s.append(kf.reshape(db, t_new, N_HEADS, HEAD_DIM))
            v_s.append(vf.reshape(db, t_new, N_HEADS, HEAD_DIM))
            ki_s.append(kif.reshape(db, t_new, IDX_DIM))
        else:
            c = i // n_mixers
            w_dw = jnp.pad(conv_w_dw[c], ((0, CONV_HALO - CONV_WIDTH), (0, 0)))
            conv = (w_dw, row(conv_b_dw[c]), row(conv_ln_g[c]), row(conv_ln_b[c]))
            w_out, b_out = bf(conv_w_out[c]), row(conv_b_out[c])

            hp, glu = _stage_a_conv(hp, ffn1, g_mix, bf(conv_w_in[c]), row(conv_b_in[c]), tile_p)
            mp = _conv_prompt(glu, *conv, batch, seq)
            conv_p.append(glu.reshape(batch, seq, d)[:, seq - CONV_CTX:])

            hs, glu = _stage_a_conv(hs, ffn1, g_mix, bf(conv_w_in[c]), row(conv_b_in[c]), tile_s)
            ms = _conv_sample(state_conv[c], glu, *conv)
            conv_s.append(jnp.concatenate([state_conv[c], glu[:, None, :]], axis=1)[:, -CONV_CTX:])

        hp = _stage_c(hp, mp, p_prompt[i].reshape(n_p, -1), w_out, b_out, ffn2, ple, tile_p)
        hs = _stage_c(hs, ms, p_sample[i].reshape(db, -1), w_out, b_out, ffn2, ple, tile_s)

    return (hp.reshape(batch, seq, d), hs.reshape(db, t_new, d), jnp.stack(k_p), jnp.stack(v_p), jnp.stack(ki_p),
            jnp.stack(conv_p), jnp.stack(k_s), jnp.stack(v_s), jnp.stack(ki_s), jnp.stack(conv_s))
```

```python
import functools

import jax
import jax.numpy as jnp
from jax import lax
from jax.experimental import pallas as pl
from jax.experimental.pallas import tpu as pltpu

F32 = jnp.float32
BF16 = jnp.bfloat16

N_HEADS = 8
HEAD_DIM = 128
IDX_HEADS = 8
IDX_DIM = 128
TOPK_MAX = 256
PAGE_SIZE = 128
CONV_WIDTH = 31
CONV_CTX = CONV_WIDTH - 1
EPS = 1e-6

LANES = 128
SUBLANES = 8
VMEM_LIMIT_BYTES = 58 * 1024 * 1024
ROW_TILE = 512
FF_CHUNK = 256
ATTN_Q_TILE = 256
ATTN_K_CHUNK = 512
CONV_T_TILE = 256
CONV_HALO = 32
CONV_ROW_BLOCK = 32
N_BISECT = 15
FOLD_ROWS = 32
LOG2_E = 1.4426950408889634
V_ROWS = HEAD_DIM + 16
BISECT_STEPS_PER_CHECK = 3
IDX_PAGES_PER_STEP = 16
KV_PAGES_PER_STEP = 8

_NT = (((1,), (1,)), ((), ()))


def _compiler_params(semantics):
    return pltpu.CompilerParams(dimension_semantics=semantics, vmem_limit_bytes=VMEM_LIMIT_BYTES)


def _resident(shape):
    nd = len(shape)
    return pl.BlockSpec(shape, lambda *_: (0,) * nd, pipeline_mode=pl.Buffered(1))


def _rows(tile, cols):
    return pl.BlockSpec((tile, cols), lambda i: (i, 0))


def _rms(x, g):
    return x * lax.rsqrt(jnp.mean(x * x, axis=-1, keepdims=True) + EPS) * g


def _dot(a, b):
    return jnp.dot(a, b, preferred_element_type=F32)


def _dot_nt(a, b):
    return lax.dot_general(a, b, _NT, preferred_element_type=F32)


def _ffn_half_step(x, g_ref, wg_ref, wu_ref, wd_ref, act_ref):
    hn = _rms(x, g_ref[...]).astype(BF16)
    d_ff = wg_ref.shape[1]
    for c in range(d_ff // FF_CHUNK):
        sl = slice(c * FF_CHUNK, (c + 1) * FF_CHUNK)
        g = _dot(hn, wg_ref[:, sl])
        u = _dot(hn, wu_ref[:, sl])
        act_ref[:, sl] = (g * jax.nn.sigmoid(g) * u).astype(BF16)
    return x + 0.5 * _dot(act_ref[...], wd_ref[...])


def _stage_a_attn_kernel(x_ref, g1_ref, wg_ref, wu_ref, wd_ref, gm_ref, win_ref, wvt_ref, wwi_ref, qg_ref, kg_ref,
                         kig_ref,
                         h_ref, qb_ref, kf_ref, vf_ref, kif_ref, kb_ref, vt_ref, qib_ref, kib_ref, wi_ref,
                         act_ref):
    h = _ffn_half_step(x_ref[...], g1_ref, wg_ref, wu_ref, wd_ref, act_ref)
    h_ref[...] = h
    hn = _rms(h, gm_ref[...]).astype(BF16)

    attn_dim = N_HEADS * HEAD_DIM
    o_qi = 3 * attn_dim
    o_ki = o_qi + IDX_HEADS * IDX_DIM
    wq_ref = win_ref.at[:, 0:attn_dim]
    wk_ref = win_ref.at[:, attn_dim:2 * attn_dim]
    wv_ref = win_ref.at[:, 2 * attn_dim:o_qi]
    wqi_ref = win_ref.at[:, o_qi:o_ki]
    wki_ref = win_ref.at[:, o_ki:o_ki + IDX_DIM]

    zq = _dot(hn, wq_ref[...])
    for hh in range(N_HEADS):
        sl = slice(hh * HEAD_DIM, (hh + 1) * HEAD_DIM)
        qb_ref[:, sl] = (_rms(zq[:, sl], qg_ref[...]) * (HEAD_DIM ** -0.5 * LOG2_E)).astype(BF16)
    zk = _dot(hn, wk_ref[...])
    for hh in range(N_HEADS):
        sl = slice(hh * HEAD_DIM, (hh + 1) * HEAD_DIM)
        kn = _rms(zk[:, sl], kg_ref[...])
        kf_ref[:, sl] = kn
        kb_ref[:, sl] = kn.astype(BF16)
    vf_ref[...] = _dot(hn, wv_ref[...])
    vt = _dot_nt(wvt_ref[...], hn).astype(BF16)
    for hh in range(N_HEADS):
        vt_ref[hh * V_ROWS:hh * V_ROWS + HEAD_DIM, :] = vt[hh * HEAD_DIM:(hh + 1) * HEAD_DIM, :]
        vt_ref[hh * V_ROWS + HEAD_DIM:(hh + 1) * V_ROWS, :] = jnp.ones((V_ROWS - HEAD_DIM, vt.shape[1]), BF16)
    qib_ref[...] = _dot(hn, wqi_ref[...]).astype(BF16)
    kin = _rms(_dot(hn, wki_ref[...]), kig_ref[...])
    kif_ref[...] = kin
    kib_ref[...] = kin.astype(BF16)
    wi_ref[...] = _dot(hn, wwi_ref[...])


def _stage_a_attn(x, ffn, g_mix, proj, tile, batch):
    n, d = x.shape
    g1, wg, wu, wd = ffn
    w_in, wvt, wwi, qg, kg, kig = proj
    d_ff = wg.shape[1]
    consts = (g1, wg, wu, wd, g_mix, w_in, wvt, wwi, qg, kg, kig)
    sds = jax.ShapeDtypeStruct
    seq = n // batch
    tiles_per_seq = seq // tile
    out_shape = (sds((n, d), F32), sds((n, d), BF16), sds((n, d), F32), sds((n, d), F32),
                 sds((n, IDX_DIM), F32), sds((n, d), BF16), sds((batch, N_HEADS * V_ROWS, seq), BF16), sds((n, d), BF16),
                 sds((n, IDX_DIM), BF16), sds((n, LANES), F32))
    out_specs = [_rows(tile, s.shape[1]) for s in out_shape]
    out_specs[6] = pl.BlockSpec((None, N_HEADS * V_ROWS, tile),
                                lambda i: (i // tiles_per_seq, 0, i % tiles_per_seq))
    return pl.pallas_call(
        _stage_a_attn_kernel,
        grid=(n // tile,),
        in_specs=[_rows(tile, d)] + [_resident(c.shape) for c in consts],
        out_specs=out_specs,
        out_shape=out_shape,
        scratch_shapes=[pltpu.VMEM((tile, d_ff), BF16)],
        compiler_params=_compiler_params(("parallel",)),
        name="stage_a_attn",
    )(x, *consts)


def _stage_a_conv_kernel(x_ref, g1_ref, wg_ref, wu_ref, wd_ref, gm_ref, wci_ref, bci_ref,
                         h_ref, glu_ref, act_ref):
    h = _ffn_half_step(x_ref[...], g1_ref, wg_ref, wu_ref, wd_ref, act_ref)
    h_ref[...] = h
    hn = _rms(h, gm_ref[...]).astype(BF16)
    d = h.shape[1]
    u = _dot(hn, wci_ref[...]) + bci_ref[...]
    glu_ref[...] = u[:, :d] * jax.nn.sigmoid(u[:, d:])


def _stage_a_conv(x, ffn, g_mix, wci, bci, tile):
    n, d = x.shape
    g1, wg, wu, wd = ffn
    consts = (g1, wg, wu, wd, g_mix, wci, bci)
    out_shape = (jax.ShapeDtypeStruct((n, d), F32), jax.ShapeDtypeStruct((n, d), F32))
    return pl.pallas_call(
        _stage_a_conv_kernel,
        grid=(n // tile,),
        in_specs=[_rows(tile, d)] + [_resident(c.shape) for c in consts],
        out_specs=[_rows(tile, d), _rows(tile, d)],
        out_shape=out_shape,
        scratch_shapes=[pltpu.VMEM((tile, wg.shape[1]), BF16)],
        compiler_params=_compiler_params(("parallel",)),
        name="stage_a_conv",
    )(x, *consts)


def _stage_c_kernel(h_ref, m_ref, p_ref, wo_ref, bo_ref, g2_ref, wg_ref, wu_ref, wd_ref, gp_ref, wpg_ref,
                    wpp_ref, out_ref, act_ref):
    h = h_ref[...] + _dot(m_ref[...], wo_ref[...]) + bo_ref[...]
    h = _ffn_half_step(h, g2_ref, wg_ref, wu_ref, wd_ref, act_ref)
    gate = jax.nn.sigmoid(_dot(_rms(h, gp_ref[...]).astype(BF16), wpg_ref[...]))
    out_ref[...] = h + gate * _dot(p_ref[...].astype(BF16), wpp_ref[...])


def _stage_c(h, m, p_all, layer, wo, bo, ffn, ple, tile):
    n, d = h.shape
    g2, wg, wu, wd = ffn
    gp, wpg, wpp = ple
    consts = (wo, bo, g2, wg, wu, wd, gp, wpg, wpp)
    p_spec = pl.BlockSpec((None, tile, p_all.shape[2]), lambda i: (layer, i, 0))
    return pl.pallas_call(
        _stage_c_kernel,
        grid=(n // tile,),
        in_specs=[_rows(tile, d), _rows(tile, d), p_spec] + [_resident(c.shape) for c in consts],
        out_specs=_rows(tile, d),
        out_shape=jax.ShapeDtypeStruct((n, d), F32),
        scratch_shapes=[pltpu.VMEM((tile, wg.shape[1]), BF16)],
        compiler_params=_compiler_params(("parallel",)),
        name="stage_c",
    )(h, m, p_all, *consts)


def _any(flag):
    return jnp.max(jnp.where(flag, 1.0, 0.0)).astype(jnp.int32)


def _topk_bracket(count_ge, max_below, row_min, row_max, n_valid, kk):
    def update(lo, hi, clo, chi, p, c):
        pend = clo != kk
        up = pend & (c >= kk)
        dn = pend & (c < kk)
        return (jnp.where(up, p, lo), jnp.where(dn, p, hi), jnp.where(up, c, clo), jnp.where(dn, c, chi))

    state = update(row_min, jnp.full_like(row_min, jnp.inf), n_valid, jnp.zeros_like(row_min),
                   row_max, count_ge(row_max))

    def bisect_cond(carry):
        return (carry[0] < N_BISECT) & (carry[1] > 0)

    def bisect_body(carry):
        it, _, *st = carry
        for _ in range(BISECT_STEPS_PER_CHECK):
            p = 0.5 * st[0] + 0.5 * st[1]
            st = update(*st, p, count_ge(p))
        return (it + BISECT_STEPS_PER_CHECK, _any(st[2] != kk)) + tuple(st)

    carry = lax.while_loop(bisect_cond, bisect_body, (jnp.int32(0), _any(state[2] != kk)) + state)
    lo, hi, clo, chi = carry[2:]
    tied = jnp.zeros_like(lo)

    def peel_cond(carry):
        return carry[0] > 0

    def peel_body(carry):
        _, lo, hi, clo, chi, tied = carry
        m = max_below(hi)
        c = count_ge(m)
        pend = (clo != kk) & (tied == 0.0)
        fin = pend & (c >= kk)
        dn = pend & (c < kk)
        lo = jnp.where(fin, m, lo)
        clo = jnp.where(fin, c, clo)
        tied = jnp.where(fin, 1.0, tied)
        hi = jnp.where(dn, m, hi)
        chi = jnp.where(dn, c, chi)
        return (_any((clo != kk) & (tied == 0.0)), lo, hi, clo, chi, tied)

    carry = lax.while_loop(peel_cond, peel_body, (_any(clo != kk), lo, hi, clo, chi, tied))
    return carry[1:5]


def _fold_rows(x, op):
    return op(x.reshape(x.shape[0] // FOLD_ROWS, FOLD_ROWS, x.shape[1]), axis=0)


def _attn_prompt_kernel(q_ref, qi_ref, wi_ref, k_ref, vt_ref, ki_ref, *rest, topk, n_stacked):
    stack_in = rest[:n_stacked]
    o_ref = rest[n_stacked]
    stack_out = rest[n_stacked + 1:n_stacked + 1 + n_stacked // 2]
    score_ref, wt_ref, acc_ref, s_ref = rest[n_stacked + 1 + n_stacked // 2:]
    for i, src_ref in enumerate(stack_in):
        stack_out[i % (n_stacked // 2)][i // (n_stacked // 2)] = src_ref[...]
    tq = q_ref.shape[0]
    seq = k_ref.shape[0]
    kc = min(ATTN_K_CHUNK, seq)
    q0 = pl.program_id(1) * tq
    n_ch = (q0 + tq + kc - 1) // kc
    q_pos = q0 + lax.broadcasted_iota(jnp.int32, (1, tq), 1)

    def chunk(c):
        return pl.ds(pl.multiple_of(c * kc, kc), kc)

    wt_ref[...] = wi_ref[...].T

    def score_body(c, carry):
        mx, mn = carry
        kic = ki_ref[chunk(c), :]
        acc = jnp.zeros((kc, tq), F32)
        for hh in range(IDX_HEADS):
            d = _dot_nt(kic, qi_ref[:, hh * IDX_DIM:(hh + 1) * IDX_DIM])
            acc = acc + wt_ref[hh:hh + 1, :] * jnp.maximum(d, 0.0)
        key_pos = c * kc + lax.broadcasted_iota(jnp.int32, (kc, 1), 0)
        allowed = key_pos <= q_pos
        sc = jnp.where(allowed, acc, -jnp.inf)
        score_ref[chunk(c), :] = sc
        mx = jnp.maximum(mx, _fold_rows(sc, jnp.max))
        mn = jnp.minimum(mn, _fold_rows(jnp.where(allowed, acc, jnp.inf), jnp.min))
        return mx, mn

    mx, mn = lax.fori_loop(0, n_ch, score_body,
                           (jnp.full((FOLD_ROWS, tq), -jnp.inf, F32), jnp.full((FOLD_ROWS, tq), jnp.inf, F32)))
    row_max = jnp.max(mx, axis=0, keepdims=True)
    row_min = jnp.min(mn, axis=0, keepdims=True)

    def count_ge(p):
        def body(c, acc):
            return acc + _fold_rows(jnp.where(score_ref[chunk(c), :] >= p, 1.0, 0.0), jnp.sum)
        acc = lax.fori_loop(0, n_ch, body, jnp.zeros((FOLD_ROWS, tq), F32))
        return jnp.sum(acc, axis=0, keepdims=True)

    def max_below(hi):
        def body(c, acc):
            x = score_ref[chunk(c), :]
            return jnp.maximum(acc, _fold_rows(jnp.where(x < hi, x, -jnp.inf), jnp.max))
        acc = lax.fori_loop(0, n_ch, body, jnp.full((FOLD_ROWS, tq), -jnp.inf, F32))
        return jnp.max(acc, axis=0, keepdims=True)

    n_valid = (q_pos + 1).astype(F32)
    kk = jnp.minimum(n_valid, float(topk))
    lo, hi, clo, chi = _topk_bracket(count_ge, max_below, row_min, row_max, n_valid, kk)
    has_ties = _any(clo != kk) > 0

    @pl.when(jnp.logical_not(has_ties))
    def _():
        def body(c, carry):
            score_ref[chunk(c), :] = jnp.where(score_ref[chunk(c), :] >= lo, 0.0, -jnp.inf)
            return carry
        lax.fori_loop(0, n_ch, body, 0)

    @pl.when(has_ties)
    def _():
        need = kk - chi
        upto = jnp.where(lax.broadcasted_iota(jnp.int32, (kc, kc), 1) <= lax.broadcasted_iota(jnp.int32, (kc, kc), 0),
                         1.0, 0.0).astype(BF16)

        def body(c, before):
            x = score_ref[chunk(c), :]
            cand = jnp.where((x >= lo) & (x < hi), 1.0, 0.0)
            rank = before + _dot(upto, cand.astype(BF16))
            take = (cand > 0.0) & (rank <= need)
            score_ref[chunk(c), :] = jnp.where((x >= hi) | take, 0.0, -jnp.inf)
            return before + jnp.sum(cand, axis=0, keepdims=True)
        lax.fori_loop(0, n_ch, body, jnp.zeros((1, tq), F32))

    acc_ref[...] = jnp.zeros(acc_ref.shape, F32)

    def logits(c, slot, m_old):
        bias = score_ref[chunk(c), :]
        m_loc = []
        for hh in range(N_HEADS):
            hs = slice(hh * HEAD_DIM, (hh + 1) * HEAD_DIM)
            s = _dot_nt(k_ref[chunk(c), hs], q_ref[:, hs]) + bias
            s_ref[slot, hh] = s
            m_loc.append(jnp.max(s, axis=0, keepdims=True))
        return jnp.maximum(m_old, jnp.concatenate(m_loc, axis=0))

    def softmax_pv(c, slot, m_acc, m_cur):
        m_safe = jnp.where(m_cur == -jnp.inf, 0.0, m_cur)
        alpha = jnp.exp2(m_acc - m_safe)
        for hh in range(N_HEADS):
            vs = slice(hh * V_ROWS, (hh + 1) * V_ROWS)
            p = jnp.exp2(s_ref[slot, hh] - m_safe[hh:hh + 1, :])
            acc_ref[vs, :] = alpha[hh:hh + 1, :] * acc_ref[vs, :] + _dot(vt_ref[vs, chunk(c)], p.astype(BF16))

    def pair_body(i, carry):
        m_acc, m_cur = carry
        m_1 = logits(2 * i + 1, 1, m_cur)
        softmax_pv(2 * i, 0, m_acc, m_cur)
        m_2 = logits(jnp.minimum(2 * i + 2, n_ch - 1), 0, m_1)
        softmax_pv(2 * i + 1, 1, m_cur, m_1)
        return m_1, m_2

    no_max = jnp.full((N_HEADS, tq), -jnp.inf, F32)
    m_acc, m_cur = lax.fori_loop(0, n_ch // 2, pair_body, (no_max, logits(0, 0, no_max)))

    @pl.when(n_ch % 2 == 1)
    def _():
        softmax_pv(n_ch - 1, 0, m_acc, m_cur)
    for hh in range(N_HEADS):
        r0 = hh * V_ROWS
        out = acc_ref[r0:r0 + HEAD_DIM, :] / acc_ref[r0 + HEAD_DIM:r0 + HEAD_DIM + 1, :]
        o_ref[:, hh * HEAD_DIM:(hh + 1) * HEAD_DIM] = out.T.astype(BF16)


def _attn_prompt(qb, qib, wi, kb, vt, kib, batch, seq, topk, stacked=()):
    n, d = qb.shape
    tq = min(ATTN_Q_TILE, seq)
    kc = min(ATTN_K_CHUNK, seq)
    nq = seq // tq
    n_layers = 2
    n_arrays = len(stacked) // n_layers
    tile_map = lambda b, j: (b * nq + j, 0)
    batch_map = lambda b, j: (b, 0)
    stack_spec = pl.BlockSpec((n_layers, tq, d), lambda b, j: (0, b * nq + j, 0))
    outs = pl.pallas_call(
        functools.partial(_attn_prompt_kernel, topk=topk, n_stacked=len(stacked)),
        grid=(batch, nq),
        in_specs=[pl.BlockSpec((tq, d), tile_map), pl.BlockSpec((tq, d), tile_map),
                  pl.BlockSpec((tq, LANES), tile_map),
                  pl.BlockSpec((seq, d), batch_map, pipeline_mode=pl.Buffered(1)),
                  pl.BlockSpec((None, N_HEADS * V_ROWS, seq), lambda b, j: (b, 0, 0), pipeline_mode=pl.Buffered(1)),
                  pl.BlockSpec((seq, IDX_DIM), batch_map, pipeline_mode=pl.Buffered(1))]
                 + [pl.BlockSpec((tq, d), tile_map)] * len(stacked),
        out_specs=[pl.BlockSpec((tq, d), tile_map)] + [stack_spec] * n_arrays,
        out_shape=[jax.ShapeDtypeStruct((n, d), BF16)] + [jax.ShapeDtypeStruct((n_layers, n, d), F32)] * n_arrays,
        scratch_shapes=[pltpu.VMEM((seq, tq), F32), pltpu.VMEM((LANES, tq), F32),
                        pltpu.VMEM((N_HEADS * V_ROWS, tq), F32), pltpu.VMEM((2, N_HEADS, kc, tq), F32)],
        compiler_params=_compiler_params(("parallel", "arbitrary")),
        name="attn_prompt",
    )(qb, qib, wi, kb, vt, kib, *stacked)
    return outs if stacked else outs[0]


def _sample_scores_kernel(pt_ref, qi_ref, w_ref, kin_ref, *rest, pages_per_step):
    del pt_ref
    page_refs = rest[:pages_per_step]
    score_ref, self_ref = rest[pages_per_step:]
    qi = qi_ref[...]
    w = w_ref[...]

    rows = []
    for r in range(pages_per_step):
        d = _dot_nt(qi, page_refs[r][...].astype(BF16))
        rows.append(jnp.sum(w * jnp.maximum(d, 0.0), axis=0, keepdims=True))
    score_ref[...] = jnp.concatenate(rows, axis=0)

    @pl.when(pl.program_id(1) == 0)
    def _():
        d_self = jnp.sum(qi.astype(F32) * kin_ref[...].astype(BF16).astype(F32), axis=1, keepdims=True)
        self_ref[...] = jnp.sum(w * jnp.maximum(d_self, 0.0), axis=0, keepdims=True)


def _sample_scores(qi8, w8, ki_new, cache_kidx, page_table, layer):
    db, n_pages = page_table.shape
    pps = min(IDX_PAGES_PER_STEP, n_pages)
    steps = n_pages // pps

    def page_spec(r):
        return pl.BlockSpec((None, None, PAGE_SIZE, IDX_DIM), lambda b, g, pt: (layer, pt[b, g * pps + r], 0, 0))

    per_seq = lambda b, g, pt: (b, 0, 0)
    grid_spec = pltpu.PrefetchScalarGridSpec(
        num_scalar_prefetch=1,
        grid=(db, steps),
        in_specs=[pl.BlockSpec((None, IDX_HEADS, IDX_DIM), per_seq), pl.BlockSpec((None, IDX_HEADS, LANES), per_seq),
                  pl.BlockSpec((None, 1, IDX_DIM), per_seq)] + [page_spec(r) for r in range(pps)],
        out_specs=[pl.BlockSpec((None, pps, PAGE_SIZE), lambda b, g, pt: (b, g, 0)),
                   pl.BlockSpec((None, 1, LANES), per_seq)],
    )
    return pl.pallas_call(
        functools.partial(_sample_scores_kernel, pages_per_step=pps),
        grid_spec=grid_spec,
        out_shape=(jax.ShapeDtypeStruct((db, n_pages, PAGE_SIZE), F32), jax.ShapeDtypeStruct((db, 1, LANES), F32)),
        compiler_params=_compiler_params(("parallel", "arbitrary")),
        name="sample_scores",
    )(page_table, qi8, w8, ki_new, *([cache_kidx] * pps))


def _sample_select_kernel(x_ref, xself_ref, sel_ref, selself_ref, *, topk):
    x = x_ref[...]
    x_self = xself_ref[...][:, :, :1]
    db, n_pages, ps = x.shape

    def total(v, op):
        return op(op(v, axis=1, keepdims=True), axis=2, keepdims=True)

    def count_ge(p):
        return total(jnp.where(x >= p, 1.0, 0.0), jnp.sum) + jnp.where(x_self >= p, 1.0, 0.0)

    def max_below(hi):
        return jnp.maximum(total(jnp.where(x < hi, x, -jnp.inf), jnp.max), jnp.where(x_self < hi, x_self, -jnp.inf))

    row_max = jnp.maximum(total(x, jnp.max), x_self)
    row_min = jnp.minimum(total(x, jnp.min), x_self)
    n_valid = jnp.full((db, 1, 1), float(n_pages * ps + 1), F32)
    kk = jnp.full((db, 1, 1), float(topk), F32)
    lo, hi, clo, chi = _topk_bracket(count_ge, max_below, row_min, row_max, n_valid, kk)
    has_ties = _any(clo != kk) > 0

    @pl.when(jnp.logical_not(has_ties))
    def _():
        sel_ref[...] = jnp.where(x >= lo, 1.0, 0.0)
        selself_ref[...] = jnp.broadcast_to(jnp.where(x_self >= lo, 1.0, 0.0), selself_ref.shape)

    @pl.when(has_ties)
    def _():
        need = kk - chi
        cand = jnp.where((x >= lo) & (x < hi), 1.0, 0.0)
        upto = jnp.where(lax.broadcasted_iota(jnp.int32, (ps, ps), 0) <= lax.broadcasted_iota(jnp.int32, (ps, ps), 1),
                         1.0, 0.0).astype(BF16)
        below = jnp.where(lax.broadcasted_iota(jnp.int32, (n_pages, n_pages), 1)
                          < lax.broadcasted_iota(jnp.int32, (n_pages, n_pages), 0), 1.0, 0.0).astype(BF16)
        in_page = _dot(cand.reshape(db * n_pages, ps).astype(BF16), upto).reshape(db, n_pages, ps)
        page_tot = jnp.broadcast_to(jnp.sum(cand, axis=2, keepdims=True), cand.shape).astype(BF16)
        for b in range(db):
            rank = in_page[b] + _dot(below, page_tot[b])
            take = (cand[b] > 0.0) & (rank <= need[b])
            sel_ref[b] = jnp.where((x[b] >= hi[b]) | take, 1.0, 0.0)
        cand_self = (x_self >= lo) & (x_self < hi)
        take_self = cand_self & (total(cand, jnp.sum) + 1.0 <= need)
        selself_ref[...] = jnp.broadcast_to(jnp.where((x_self >= hi) | take_self, 1.0, 0.0), selself_ref.shape)


def _sample_select(scores, score_self, topk):
    args = (scores, score_self)
    return pl.pallas_call(
        functools.partial(_sample_select_kernel, topk=topk),
        grid=(1,),
        in_specs=[_resident(a.shape) for a in args],
        out_specs=[pl.BlockSpec(a.shape, lambda i: (0, 0, 0)) for a in args],
        out_shape=tuple(jax.ShapeDtypeStruct(a.shape, F32) for a in args),
        compiler_params=_compiler_params(("arbitrary",)),
        name="sample_select",
    )(*args)


def _sample_attn_kernel(pt_ref, q_ref, sel_ref, selself_ref, kn_ref, vn_ref, spread_ref, *rest, pages_per_step):
    del pt_ref
    k_refs = rest[:pages_per_step]
    v_refs = rest[pages_per_step:2 * pages_per_step]
    o_ref, m_ref, l_ref, acc_ref = rest[2 * pages_per_step:]
    g = pl.program_id(1)
    q = q_ref[...]
    q_bf = q.astype(BF16)
    cols = PAGE_SIZE * N_HEADS
    own = (lax.broadcasted_iota(jnp.int32, (N_HEADS, cols), 1) % N_HEADS
           == lax.broadcasted_iota(jnp.int32, (N_HEADS, cols), 0))

    @pl.when(g == 0)
    def _():
        m_ref[...] = jnp.full(m_ref.shape, -jnp.inf, F32)
        l_ref[...] = jnp.zeros(l_ref.shape, F32)
        acc_ref[...] = jnp.zeros(acc_ref.shape, F32)

    def accumulate(logits, value_fns):
        m_old = m_ref[...]
        m_new = m_old
        for s in logits:
            m_new = jnp.maximum(m_new, jnp.max(s, axis=1, keepdims=True))
        m_safe = jnp.where(m_new == -jnp.inf, 0.0, m_new)
        alpha = jnp.exp2(m_old - m_safe)
        l = alpha * l_ref[...]
        acc = alpha * acc_ref[...]
        for s, value_fn in zip(logits, value_fns):
            p = jnp.exp2(s - m_safe)
            l = l + jnp.sum(p, axis=1, keepdims=True)
            acc = acc + value_fn(p)
        l_ref[...] = l
        acc_ref[...] = acc
        m_ref[...] = m_new

    sel_cols = _dot(sel_ref[...].astype(BF16), spread_ref[...])
    logits = []
    for r in range(pages_per_step):
        s = _dot_nt(q_bf, k_refs[r][...].astype(BF16))
        logits.append(jnp.where(own & (sel_cols[r:r + 1, :] > 0.0), s, -jnp.inf))
    accumulate(logits, [lambda p, r=r: _dot(p.astype(BF16), v_refs[r][...].astype(BF16))
                        for r in range(pages_per_step)])

    @pl.when(g == pl.num_programs(1) - 1)
    def _():
        kn = kn_ref[...].astype(BF16).astype(F32)
        s_self = jnp.sum(q * kn, axis=1, keepdims=True)
        s_self = jnp.where(selself_ref[:, :1] > 0.0, s_self, -jnp.inf)
        vn = vn_ref[...].astype(BF16).astype(F32)
        accumulate([s_self], [lambda p: p.astype(BF16).astype(F32) * vn])
        o_ref[...] = acc_ref[...] / l_ref[...]


def _sample_attn(q, sel, sel_self, k_new, v_new, cache_k, cache_v, page_table, layer):
    db, n_pages = page_table.shape
    pps = min(KV_PAGES_PER_STEP, n_pages)
    steps = n_pages // pps
    cols = PAGE_SIZE * N_HEADS
    sel4 = sel.reshape(db, steps, pps, PAGE_SIZE)
    spread = (jnp.arange(cols)[None, :] // N_HEADS == jnp.arange(PAGE_SIZE)[:, None]).astype(BF16)

    def page_spec(r):
        return pl.BlockSpec((None, None, cols, HEAD_DIM), lambda b, g, pt: (layer, pt[b, g * pps + r], 0, 0))

    per_seq = lambda b, g, pt: (b, 0, 0)
    heads_spec = pl.BlockSpec((None, N_HEADS, HEAD_DIM), per_seq)
    grid_spec = pltpu.PrefetchScalarGridSpec(
        num_scalar_prefetch=1,
        grid=(db, steps),
        in_specs=[heads_spec,
                  pl.BlockSpec((None, None, pps, PAGE_SIZE), lambda b, g, pt: (b, g, 0, 0)),
                  pl.BlockSpec((None, 1, LANES), per_seq), heads_spec, heads_spec,
                  pl.BlockSpec((PAGE_SIZE, cols), lambda b, g, pt: (0, 0))]
                 + [page_spec(r) for r in range(pps)] * 2,
        out_specs=heads_spec,
        scratch_shapes=[pltpu.VMEM((N_HEADS, 1), F32), pltpu.VMEM((N_HEADS, 1), F32),
                        pltpu.VMEM((N_HEADS, HEAD_DIM), F32)],
    )
    return pl.pallas_call(
        functools.partial(_sample_attn_kernel, pages_per_step=pps),
        grid_spec=grid_spec,
        out_shape=jax.ShapeDtypeStruct((db, N_HEADS, HEAD_DIM), F32),
        compiler_params=_compiler_params(("parallel", "arbitrary")),
        name="sample_attn",
    )(page_table, q, sel4, sel_self, k_new, v_new, spread, *([cache_k] * pps), *([cache_v] * pps))


def _ln_swish(y, g, b):
    yc = y - jnp.mean(y, axis=-1, keepdims=True)
    yn = yc * lax.rsqrt(jnp.mean(yc * yc, axis=-1, keepdims=True) + EPS) * g + b
    return yn * jax.nn.sigmoid(yn)


def _conv_prompt_kernel(cur_ref, halo_ref, w_ref, bdw_ref, lng_ref, lnb_ref, y_ref, sh_ref, conv_ref):
    tt, d = cur_ref.shape
    sh_ref[0, :CONV_HALO, :] = jnp.where(pl.program_id(1) == 0, 0.0, halo_ref[...])
    sh_ref[0, CONV_HALO:, :] = cur_ref[...]
    n_sh = tt + CONV_HALO - SUBLANES
    for r in range(1, SUBLANES):
        sh_ref[r, :n_sh, :] = sh_ref[0, r:r + n_sh, :]
    first = CONV_HALO - CONV_CTX

    def block(rb, carry):
        r0 = pl.multiple_of(rb * CONV_ROW_BLOCK, CONV_ROW_BLOCK)
        for lt in range(d // LANES):
            ls = slice(lt * LANES, (lt + 1) * LANES)
            acc = jnp.zeros((CONV_ROW_BLOCK, LANES), F32)
            for r in range(SUBLANES):
                taps = [j for j in range(CONV_WIDTH) if (j + first) % SUBLANES == r]
                a_max = max((j + first) // SUBLANES for j in taps)
                rows = sh_ref[r, pl.ds(r0, a_max * SUBLANES + CONV_ROW_BLOCK), ls]
                for j in taps:
                    a = (j + first) // SUBLANES
                    acc = acc + w_ref[j:j + 1, ls] * rows[a * SUBLANES:a * SUBLANES + CONV_ROW_BLOCK]
            conv_ref[pl.ds(r0, CONV_ROW_BLOCK), ls] = acc
        return carry

    lax.fori_loop(0, tt // CONV_ROW_BLOCK, block, 0)
    y = conv_ref[...] + bdw_ref[...]
    y_ref[...] = _ln_swish(y, lng_ref[...], lnb_ref[...]).astype(BF16)


def _conv_prompt(glu, w_dw, b_dw, ln_g, ln_b, batch, seq):
    n, d = glu.shape
    tt = min(CONV_T_TILE, seq)
    nt = seq // tt
    halo_per_tile = tt // CONV_HALO
    halo_per_seq = seq // CONV_HALO
    tile_map = lambda b, j: (b * nt + j, 0)
    halo_map = lambda b, j: (jnp.maximum(b * halo_per_seq + j * halo_per_tile - 1, 0), 0)
    consts = (w_dw, b_dw, ln_g, ln_b)
    return pl.pallas_call(
        _conv_prompt_kernel,
        grid=(batch, nt),
        in_specs=[pl.BlockSpec((tt, d), tile_map), pl.BlockSpec((CONV_HALO, d), halo_map)]
                 + [_resident(c.shape) for c in consts],
        out_specs=pl.BlockSpec((tt, d), tile_map),
        out_shape=jax.ShapeDtypeStruct((n, d), BF16),
        scratch_shapes=[pltpu.VMEM((SUBLANES, tt + CONV_HALO, d), F32), pltpu.VMEM((tt, d), F32)],
        compiler_params=_compiler_params(("parallel", "arbitrary")),
        name="conv_prompt",
    )(glu, glu, *consts)


def _conv_sample_kernel(state_ref, u_ref, w_ref, bdw_ref, lng_ref, lnb_ref, y_ref, conv_ref):
    db = u_ref.shape[0]
    w_ctx = w_ref[:CONV_CTX, :]
    for b in range(db):
        conv_ref[b:b + 1, :] = jnp.sum(state_ref[b] * w_ctx, axis=0, keepdims=True)
    y = conv_ref[...] + u_ref[...] * w_ref[CONV_CTX:CONV_CTX + 1, :] + bdw_ref[...]
    y_ref[...] = _ln_swish(y, lng_ref[...], lnb_ref[...]).astype(BF16)


def _conv_sample(state_all, layer, u, w_dw, b_dw, ln_g, ln_b):
    db, d = u.shape
    args = (state_all, u, w_dw, b_dw, ln_g, ln_b)
    state_spec = pl.BlockSpec((None,) + state_all.shape[1:], lambda i: (layer, 0, 0, 0),
                              pipeline_mode=pl.Buffered(1))
    return pl.pallas_call(
        _conv_sample_kernel,
        grid=(1,),
        in_specs=[state_spec] + [_resident(a.shape) for a in args[1:]],
        out_specs=pl.BlockSpec((db, d), lambda i: (0, 0)),
        out_shape=jax.ShapeDtypeStruct((db, d), BF16),
        scratch_shapes=[pltpu.VMEM((db, d), F32)],
        compiler_params=_compiler_params(("arbitrary",)),
        name="conv_sample",
    )(*args)


def kernel(x_prompt, x_sample, p_prompt, p_sample, cache_k, cache_v, cache_kidx, state_conv, page_table, ffn1_norm, ffn1_w_gate, ffn1_w_up, ffn1_w_down, mix_norm, attn_w_in, attn_q_gain, attn_k_gain, attn_kidx_gain, attn_w_o, conv_w_in, conv_b_in, conv_w_dw, conv_b_dw, conv_ln_g, conv_ln_b, conv_w_out, conv_b_out, ffn2_norm, ffn2_w_gate, ffn2_w_up, ffn2_w_down, ple_norm, ple_w_gate, ple_w_proj):
    batch, seq, d = x_prompt.shape
    db, t_new, _ = x_sample.shape
    depth = ffn1_norm.shape[0]
    n_mixers = 2
    n_pages = page_table.shape[1]
    past = n_pages * PAGE_SIZE
    attn_dim = N_HEADS * HEAD_DIM
    assert t_new == 1 and d == attn_dim
    topk_p = min(TOPK_MAX, seq // 4)
    topk_s = min(TOPK_MAX, (past + t_new) // 4)
    n_p = batch * seq
    tile_p = min(ROW_TILE, seq)
    tile_s = db

    row = lambda v: v.reshape(1, -1)
    bf = lambda w: w.astype(BF16)
    hp = x_prompt.reshape(n_p, d)
    hs = x_sample.reshape(db, d)
    cache_k4 = cache_k.reshape(cache_k.shape[0], cache_k.shape[1], PAGE_SIZE * N_HEADS, HEAD_DIM)
    cache_v4 = cache_v.reshape(cache_v.shape[0], cache_v.shape[1], PAGE_SIZE * N_HEADS, HEAD_DIM)
    zero_bias = jnp.zeros((1, d), F32)
    n_attn = attn_w_in.shape[0]
    pp_all = p_prompt.reshape(depth, n_p, -1)
    ps_all = p_sample.reshape(depth, db, -1)

    k_p, v_p, ki_p, conv_p = [], [], [], []
    k_s, v_s, ki_s, conv_s = [], [], [], []
    for i in range(depth):
        ffn1 = (row(ffn1_norm[i]), bf(ffn1_w_gate[i]), bf(ffn1_w_up[i]), bf(ffn1_w_down[i]))
        ffn2 = (row(ffn2_norm[i]), bf(ffn2_w_gate[i]), bf(ffn2_w_up[i]), bf(ffn2_w_down[i]))
        ple = (row(ple_norm[i]), bf(ple_w_gate[i]), bf(ple_w_proj[i]))
        g_mix = row(mix_norm[i])
        if i % n_mixers == 0:
            a = i // n_mixers
            w_in = bf(attn_w_in[a])
            o_v, o_wi = 2 * attn_dim, 3 * attn_dim + IDX_HEADS * IDX_DIM + IDX_DIM
            w_wi = jnp.pad(w_in[:, o_wi:], ((0, 0), (0, LANES - IDX_HEADS)))
            proj = (w_in, w_in[:, o_v:o_v + attn_dim].T, w_wi, row(attn_q_gain[a]), row(attn_k_gain[a]),
                    row(attn_kidx_gain[a]))
            w_out, b_out = bf(attn_w_o[a]), zero_bias

            hp, qb, kf, vf, kif, kb, vt, qib, kib, wi = _stage_a_attn(hp, ffn1, g_mix, proj, tile_p, batch)
            k_p.append(kf)
            v_p.append(vf)
            ki_p.append(kif.reshape(batch, seq, IDX_DIM))
            if n_attn == 2 and a == n_attn - 1:
                mp, k_stack, v_stack = _attn_prompt(qb, qib, wi, kb, vt, kib, batch, seq, topk_p,
                                                    stacked=(k_p[0], v_p[0], k_p[1], v_p[1]))
            else:
                mp = _attn_prompt(qb, qib, wi, kb, vt, kib, batch, seq, topk_p)

            hs, qb, kf, vf, kif, kb, vt, qib, kib, wi = _stage_a_attn(hs, ffn1, g_mix, proj, tile_s, 1)
            qi8 = qib.reshape(db, IDX_HEADS, IDX_DIM)
            w8 = jnp.broadcast_to(wi[:, :IDX_HEADS, None], (db, IDX_HEADS, LANES))
            scores, score_self = _sample_scores(qi8, w8, kif.reshape(db, 1, IDX_DIM), cache_kidx, page_table, a)
            sel, sel_self = _sample_select(scores, score_self, topk_s)
            per_head = lambda v: v.astype(F32).reshape(db, N_HEADS, HEAD_DIM)
            ms = _sample_attn(per_head(qb), sel, sel_self, per_head(kf), per_head(vf), cache_k4, cache_v4,
                              page_table, a)
            ms = ms.reshape(db, d).astype(BF16)
            k_s.append(kf.reshape(db, t_new, N_HEADS, HEAD_DIM))
            v_s.append(vf.reshape(db, t_new, N_HEADS, HEAD_DIM))
            ki_s.append(kif.reshape(db, t_new, IDX_DIM))
        else:
            c = i // n_mixers
            w_dw = jnp.pad(conv_w_dw[c], ((0, CONV_HALO - CONV_WIDTH), (0, 0)))
            conv = (w_dw, row(conv_b_dw[c]), row(conv_ln_g[c]), row(conv_ln_b[c]))
            w_out, b_out = bf(conv_w_out[c]), row(conv_b_out[c])

            hp, glu = _stage_a_conv(hp, ffn1, g_mix, bf(conv_w_in[c]), row(conv_b_in[c]), tile_p)
            mp = _conv_prompt(glu, *conv, batch, seq)
            conv_p.append(glu.reshape(batch, seq, d)[:, seq - CONV_CTX:])

            hs, glu = _stage_a_conv(hs, ffn1, g_mix, bf(conv_w_in[c]), row(conv_b_in[c]), tile_s)
            ms = _conv_sample(state_conv, c, glu, *conv)
            conv_s.append(jnp.concatenate([state_conv[c], glu[:, None, :]], axis=1)[:, -CONV_CTX:])

        hp = _stage_c(hp, mp, pp_all, i, w_out, b_out, ffn2, ple, tile_p)
        hs = _stage_c(hs, ms, ps_all, i, w_out, b_out, ffn2, ple, tile_s)

    if n_attn != 2:
        k_stack, v_stack = jnp.stack(k_p), jnp.stack(v_p)
    kv_shape = (n_attn, batch, seq, N_HEADS, HEAD_DIM)
    return (hp.reshape(batch, seq, d), hs.reshape(db, t_new, d), k_stack.reshape(kv_shape), v_stack.reshape(kv_shape),
            jnp.stack(ki_p),
            jnp.stack(conv_p), jnp.stack(k_s), jnp.stack(v_s), jnp.stack(ki_s), jnp.stack(conv_s))
```

```python
import functools

import jax
import jax.numpy as jnp
from jax import lax
from jax.experimental import pallas as pl
from jax.experimental.pallas import tpu as pltpu

F32 = jnp.float32
BF16 = jnp.bfloat16

N_HEADS = 8
HEAD_DIM = 128
IDX_HEADS = 8
IDX_DIM = 128
TOPK_MAX = 256
PAGE_SIZE = 128
CONV_WIDTH = 31
CONV_CTX = CONV_WIDTH - 1
EPS = 1e-6

LANES = 128
SUBLANES = 8
VMEM_LIMIT_BYTES = 58 * 1024 * 1024
ROW_TILE = 512
FF_CHUNK = 256
ATTN_Q_TILE = 256
ATTN_K_CHUNK = 512
CONV_T_TILE = 256
CONV_HALO = 32
CONV_ROW_BLOCK = 32
N_BISECT = 15
FOLD_ROWS = 32
LOG2_E = 1.4426950408889634
V_ROWS = HEAD_DIM + 16
BISECT_STEPS_PER_CHECK = 3
IDX_PAGES_PER_STEP = 32
KV_PAGES_PER_STEP = 16

_NT = (((1,), (1,)), ((), ()))


def _compiler_params(semantics):
    return pltpu.CompilerParams(dimension_semantics=semantics, vmem_limit_bytes=VMEM_LIMIT_BYTES)


def _resident(shape):
    nd = len(shape)
    return pl.BlockSpec(shape, lambda *_: (0,) * nd, pipeline_mode=pl.Buffered(1))


def _rows(tile, cols):
    return pl.BlockSpec((tile, cols), lambda i: (i, 0))


def _rms(x, g):
    return x * lax.rsqrt(jnp.mean(x * x, axis=-1, keepdims=True) + EPS) * g


def _dot(a, b):
    return jnp.dot(a, b, preferred_element_type=F32)


def _dot_nt(a, b):
    return lax.dot_general(a, b, _NT, preferred_element_type=F32)


def _ffn_half_step(x, g_ref, wg_ref, wu_ref, wd_ref, act_ref):
    hn = _rms(x, g_ref[...]).astype(BF16)
    d_ff = wg_ref.shape[1]
    for c in range(d_ff // FF_CHUNK):
        sl = slice(c * FF_CHUNK, (c + 1) * FF_CHUNK)
        g = _dot(hn, wg_ref[:, sl])
        u = _dot(hn, wu_ref[:, sl])
        act_ref[:, sl] = (g * jax.nn.sigmoid(g) * u).astype(BF16)
    return x + 0.5 * _dot(act_ref[...], wd_ref[...])


def _stage_a_attn_kernel(x_ref, g1_ref, wg_ref, wu_ref, wd_ref, gm_ref, win_ref, wvt_ref, wwi_ref, qg_ref, kg_ref,
                         kig_ref,
                         h_ref, qb_ref, kf_ref, vf_ref, kif_ref, kb_ref, vt_ref, qib_ref, kib_ref, wi_ref,
                         act_ref):
    h = _ffn_half_step(x_ref[...], g1_ref, wg_ref, wu_ref, wd_ref, act_ref)
    h_ref[...] = h
    hn = _rms(h, gm_ref[...]).astype(BF16)

    attn_dim = N_HEADS * HEAD_DIM
    o_qi = 3 * attn_dim
    o_ki = o_qi + IDX_HEADS * IDX_DIM
    wq_ref = win_ref.at[:, 0:attn_dim]
    wk_ref = win_ref.at[:, attn_dim:2 * attn_dim]
    wv_ref = win_ref.at[:, 2 * attn_dim:o_qi]
    wqi_ref = win_ref.at[:, o_qi:o_ki]
    wki_ref = win_ref.at[:, o_ki:o_ki + IDX_DIM]

    zq = _dot(hn, wq_ref[...])
    for hh in range(N_HEADS):
        sl = slice(hh * HEAD_DIM, (hh + 1) * HEAD_DIM)
        qb_ref[:, sl] = (_rms(zq[:, sl], qg_ref[...]) * (HEAD_DIM ** -0.5 * LOG2_E)).astype(BF16)
    zk = _dot(hn, wk_ref[...])
    for hh in range(N_HEADS):
        sl = slice(hh * HEAD_DIM, (hh + 1) * HEAD_DIM)
        kn = _rms(zk[:, sl], kg_ref[...])
        kf_ref[:, sl] = kn
        kb_ref[:, sl] = kn.astype(BF16)
    vf_ref[...] = _dot(hn, wv_ref[...])
    vt = _dot_nt(wvt_ref[...], hn).astype(BF16)
    for hh in range(N_HEADS):
        vt_ref[hh * V_ROWS:hh * V_ROWS + HEAD_DIM, :] = vt[hh * HEAD_DIM:(hh + 1) * HEAD_DIM, :]
        vt_ref[hh * V_ROWS + HEAD_DIM:(hh + 1) * V_ROWS, :] = jnp.ones((V_ROWS - HEAD_DIM, vt.shape[1]), BF16)
    qib_ref[...] = _dot(hn, wqi_ref[...]).astype(BF16)
    kin = _rms(_dot(hn, wki_ref[...]), kig_ref[...])
    kif_ref[...] = kin
    kib_ref[...] = kin.astype(BF16)
    wi_ref[...] = _dot(hn, wwi_ref[...])


def _stage_a_attn(x, ffn, g_mix, proj, tile, batch):
    n, d = x.shape
    g1, wg, wu, wd = ffn
    w_in, wvt, wwi, qg, kg, kig = proj
    d_ff = wg.shape[1]
    consts = (g1, wg, wu, wd, g_mix, w_in, wvt, wwi, qg, kg, kig)
    sds = jax.ShapeDtypeStruct
    seq = n // batch
    tiles_per_seq = seq // tile
    out_shape = (sds((n, d), F32), sds((n, d), BF16), sds((n, d), F32), sds((n, d), F32),
                 sds((n, IDX_DIM), F32), sds((n, d), BF16), sds((batch, N_HEADS * V_ROWS, seq), BF16), sds((n, d), BF16),
                 sds((n, IDX_DIM), BF16), sds((n, LANES), F32))
    out_specs = [_rows(tile, s.shape[1]) for s in out_shape]
    out_specs[6] = pl.BlockSpec((None, N_HEADS * V_ROWS, tile),
                                lambda i: (i // tiles_per_seq, 0, i % tiles_per_seq))
    return pl.pallas_call(
        _stage_a_attn_kernel,
        grid=(n // tile,),
        in_specs=[_rows(tile, d)] + [_resident(c.shape) for c in consts],
        out_specs=out_specs,
        out_shape=out_shape,
        scratch_shapes=[pltpu.VMEM((tile, d_ff), BF16)],
        compiler_params=_compiler_params(("parallel",)),
        name="stage_a_attn",
    )(x, *consts)


def _stage_a_conv_kernel(x_ref, g1_ref, wg_ref, wu_ref, wd_ref, gm_ref, wci_ref, bci_ref,
                         h_ref, glu_ref, act_ref):
    h = _ffn_half_step(x_ref[...], g1_ref, wg_ref, wu_ref, wd_ref, act_ref)
    h_ref[...] = h
    hn = _rms(h, gm_ref[...]).astype(BF16)
    d = h.shape[1]
    u = _dot(hn, wci_ref[...]) + bci_ref[...]
    glu_ref[...] = u[:, :d] * jax.nn.sigmoid(u[:, d:])


def _stage_a_conv(x, ffn, g_mix, wci, bci, tile):
    n, d = x.shape
    g1, wg, wu, wd = ffn
    consts = (g1, wg, wu, wd, g_mix, wci, bci)
    out_shape = (jax.ShapeDtypeStruct((n, d), F32), jax.ShapeDtypeStruct((n, d), F32))
    return pl.pallas_call(
        _stage_a_conv_kernel,
        grid=(n // tile,),
        in_specs=[_rows(tile, d)] + [_resident(c.shape) for c in consts],
        out_specs=[_rows(tile, d), _rows(tile, d)],
        out_shape=out_shape,
        scratch_shapes=[pltpu.VMEM((tile, wg.shape[1]), BF16)],
        compiler_params=_compiler_params(("parallel",)),
        name="stage_a_conv",
    )(x, *consts)


def _stage_c_kernel(h_ref, m_ref, p_ref, wo_ref, bo_ref, g2_ref, wg_ref, wu_ref, wd_ref, gp_ref, wpg_ref,
                    wpp_ref, out_ref, act_ref):
    h = h_ref[...] + _dot(m_ref[...], wo_ref[...]) + bo_ref[...]
    h = _ffn_half_step(h, g2_ref, wg_ref, wu_ref, wd_ref, act_ref)
    gate = jax.nn.sigmoid(_dot(_rms(h, gp_ref[...]).astype(BF16), wpg_ref[...]))
    out_ref[...] = h + gate * _dot(p_ref[...].astype(BF16), wpp_ref[...])


def _stage_c(h, m, p_all, layer, wo, bo, ffn, ple, tile):
    n, d = h.shape
    g2, wg, wu, wd = ffn
    gp, wpg, wpp = ple
    consts = (wo, bo, g2, wg, wu, wd, gp, wpg, wpp)
    p_spec = pl.BlockSpec((None, tile, p_all.shape[2]), lambda i: (layer, i, 0))
    return pl.pallas_call(
        _stage_c_kernel,
        grid=(n // tile,),
        in_specs=[_rows(tile, d), _rows(tile, d), p_spec] + [_resident(c.shape) for c in consts],
        out_specs=_rows(tile, d),
        out_shape=jax.ShapeDtypeStruct((n, d), F32),
        scratch_shapes=[pltpu.VMEM((tile, wg.shape[1]), BF16)],
        compiler_params=_compiler_params(("parallel",)),
        name="stage_c",
    )(h, m, p_all, *consts)


def _any(flag):
    return jnp.max(jnp.where(flag, 1.0, 0.0)).astype(jnp.int32)


def _topk_bracket(count_ge, max_below, row_min, row_max, n_valid, kk):
    def update(lo, hi, clo, chi, p, c):
        pend = clo != kk
        up = pend & (c >= kk)
        dn = pend & (c < kk)
        return (jnp.where(up, p, lo), jnp.where(dn, p, hi), jnp.where(up, c, clo), jnp.where(dn, c, chi))

    state = update(row_min, jnp.full_like(row_min, jnp.inf), n_valid, jnp.zeros_like(row_min),
                   row_max, count_ge(row_max))

    def bisect_cond(carry):
        return (carry[0] < N_BISECT) & (carry[1] > 0)

    def bisect_body(carry):
        it, _, *st = carry
        for _ in range(BISECT_STEPS_PER_CHECK):
            p = 0.5 * st[0] + 0.5 * st[1]
            st = update(*st, p, count_ge(p))
        return (it + BISECT_STEPS_PER_CHECK, _any(st[2] != kk)) + tuple(st)

    carry = lax.while_loop(bisect_cond, bisect_body, (jnp.int32(0), _any(state[2] != kk)) + state)
    lo, hi, clo, chi = carry[2:]
    tied = jnp.zeros_like(lo)

    def peel_cond(carry):
        return carry[0] > 0

    def peel_body(carry):
        _, lo, hi, clo, chi, tied = carry
        m = max_below(hi)
        c = count_ge(m)
        pend = (clo != kk) & (tied == 0.0)
        fin = pend & (c >= kk)
        dn = pend & (c < kk)
        lo = jnp.where(fin, m, lo)
        clo = jnp.where(fin, c, clo)
        tied = jnp.where(fin, 1.0, tied)
        hi = jnp.where(dn, m, hi)
        chi = jnp.where(dn, c, chi)
        return (_any((clo != kk) & (tied == 0.0)), lo, hi, clo, chi, tied)

    carry = lax.while_loop(peel_cond, peel_body, (_any(clo != kk), lo, hi, clo, chi, tied))
    return carry[1:5]


def _fold_rows(x, op):
    return op(x.reshape(x.shape[0] // FOLD_ROWS, FOLD_ROWS, x.shape[1]), axis=0)


def _attn_prompt_kernel(q_ref, qi_ref, wi_ref, k_ref, vt_ref, ki_ref, *rest, topk, n_stacked):
    stack_in = rest[:n_stacked]
    o_ref = rest[n_stacked]
    stack_out = rest[n_stacked + 1:n_stacked + 1 + n_stacked // 2]
    score_ref, wt_ref, acc_ref, s_ref = rest[n_stacked + 1 + n_stacked // 2:]
    for i, src_ref in enumerate(stack_in):
        stack_out[i % (n_stacked // 2)][i // (n_stacked // 2)] = src_ref[...]
    tq = q_ref.shape[0]
    seq = k_ref.shape[0]
    kc = min(ATTN_K_CHUNK, seq)
    q0 = pl.program_id(1) * tq
    n_ch = (q0 + tq + kc - 1) // kc
    q_pos = q0 + lax.broadcasted_iota(jnp.int32, (1, tq), 1)

    def chunk(c):
        return pl.ds(pl.multiple_of(c * kc, kc), kc)

    wt_ref[...] = wi_ref[...].T

    def score_body(c, carry):
        mx, mn = carry
        kic = ki_ref[chunk(c), :]
        acc = jnp.zeros((kc, tq), F32)
        for hh in range(IDX_HEADS):
            d = _dot_nt(kic, qi_ref[:, hh * IDX_DIM:(hh + 1) * IDX_DIM])
            acc = acc + wt_ref[hh:hh + 1, :] * jnp.maximum(d, 0.0)
        key_pos = c * kc + lax.broadcasted_iota(jnp.int32, (kc, 1), 0)
        allowed = key_pos <= q_pos
        sc = jnp.where(allowed, acc, -jnp.inf)
        score_ref[chunk(c), :] = sc
        mx = jnp.maximum(mx, _fold_rows(sc, jnp.max))
        mn = jnp.minimum(mn, _fold_rows(jnp.where(allowed, acc, jnp.inf), jnp.min))
        return mx, mn

    mx, mn = lax.fori_loop(0, n_ch, score_body,
                           (jnp.full((FOLD_ROWS, tq), -jnp.inf, F32), jnp.full((FOLD_ROWS, tq), jnp.inf, F32)))
    row_max = jnp.max(mx, axis=0, keepdims=True)
    row_min = jnp.min(mn, axis=0, keepdims=True)

    def count_ge(p):
        def body(c, acc):
            return acc + _fold_rows(jnp.where(score_ref[chunk(c), :] >= p, 1.0, 0.0), jnp.sum)
        acc = lax.fori_loop(0, n_ch, body, jnp.zeros((FOLD_ROWS, tq), F32))
        return jnp.sum(acc, axis=0, keepdims=True)

    def max_below(hi):
        def body(c, acc):
            x = score_ref[chunk(c), :]
            return jnp.maximum(acc, _fold_rows(jnp.where(x < hi, x, -jnp.inf), jnp.max))
        acc = lax.fori_loop(0, n_ch, body, jnp.full((FOLD_ROWS, tq), -jnp.inf, F32))
        return jnp.max(acc, axis=0, keepdims=True)

    n_valid = (q_pos + 1).astype(F32)
    kk = jnp.minimum(n_valid, float(topk))
    lo, hi, clo, chi = _topk_bracket(count_ge, max_below, row_min, row_max, n_valid, kk)
    has_ties = _any(clo != kk) > 0

    @pl.when(jnp.logical_not(has_ties))
    def _():
        def body(c, carry):
            score_ref[chunk(c), :] = jnp.where(score_ref[chunk(c), :] >= lo, 0.0, -jnp.inf)
            return carry
        lax.fori_loop(0, n_ch, body, 0)

    @pl.when(has_ties)
    def _():
        need = kk - chi
        upto = jnp.where(lax.broadcasted_iota(jnp.int32, (kc, kc), 1) <= lax.broadcasted_iota(jnp.int32, (kc, kc), 0),
                         1.0, 0.0).astype(BF16)

        def body(c, before):
            x = score_ref[chunk(c), :]
            cand = jnp.where((x >= lo) & (x < hi), 1.0, 0.0)
            rank = before + _dot(upto, cand.astype(BF16))
            take = (cand > 0.0) & (rank <= need)
            score_ref[chunk(c), :] = jnp.where((x >= hi) | take, 0.0, -jnp.inf)
            return before + jnp.sum(cand, axis=0, keepdims=True)
        lax.fori_loop(0, n_ch, body, jnp.zeros((1, tq), F32))

    acc_ref[...] = jnp.zeros(acc_ref.shape, F32)

    def logits(c, slot, m_old):
        bias = score_ref[chunk(c), :]
        m_loc = []
        for hh in range(N_HEADS):
            hs = slice(hh * HEAD_DIM, (hh + 1) * HEAD_DIM)
            s = _dot_nt(k_ref[chunk(c), hs], q_ref[:, hs]) + bias
            s_ref[slot, hh] = s
            m_loc.append(jnp.max(s, axis=0, keepdims=True))
        return jnp.maximum(m_old, jnp.concatenate(m_loc, axis=0))

    def softmax_pv(c, slot, m_acc, m_cur):
        m_safe = jnp.where(m_cur == -jnp.inf, 0.0, m_cur)
        alpha = jnp.exp2(m_acc - m_safe)
        for hh in range(N_HEADS):
            vs = slice(hh * V_ROWS, (hh + 1) * V_ROWS)
            p = jnp.exp2(s_ref[slot, hh] - m_safe[hh:hh + 1, :])
            acc_ref[vs, :] = alpha[hh:hh + 1, :] * acc_ref[vs, :] + _dot(vt_ref[vs, chunk(c)], p.astype(BF16))

    def pair_body(i, carry):
        m_acc, m_cur = carry
        m_1 = logits(2 * i + 1, 1, m_cur)
        softmax_pv(2 * i, 0, m_acc, m_cur)
        m_2 = logits(jnp.minimum(2 * i + 2, n_ch - 1), 0, m_1)
        softmax_pv(2 * i + 1, 1, m_cur, m_1)
        return m_1, m_2

    no_max = jnp.full((N_HEADS, tq), -jnp.inf, F32)
    m_acc, m_cur = lax.fori_loop(0, n_ch // 2, pair_body, (no_max, logits(0, 0, no_max)))

    @pl.when(n_ch % 2 == 1)
    def _():
        softmax_pv(n_ch - 1, 0, m_acc, m_cur)
    for hh in range(N_HEADS):
        r0 = hh * V_ROWS
        out = acc_ref[r0:r0 + HEAD_DIM, :] / acc_ref[r0 + HEAD_DIM:r0 + HEAD_DIM + 1, :]
        o_ref[:, hh * HEAD_DIM:(hh + 1) * HEAD_DIM] = out.T.astype(BF16)


def _attn_prompt(qb, qib, wi, kb, vt, kib, batch, seq, topk, stacked=()):
    n, d = qb.shape
    tq = min(ATTN_Q_TILE, seq)
    kc = min(ATTN_K_CHUNK, seq)
    nq = seq // tq
    n_layers = 2
    n_arrays = len(stacked) // n_layers
    tile_map = lambda b, j: (b * nq + j, 0)
    batch_map = lambda b, j: (b, 0)
    stack_spec = pl.BlockSpec((n_layers, tq, d), lambda b, j: (0, b * nq + j, 0))
    outs = pl.pallas_call(
        functools.partial(_attn_prompt_kernel, topk=topk, n_stacked=len(stacked)),
        grid=(batch, nq),
        in_specs=[pl.BlockSpec((tq, d), tile_map), pl.BlockSpec((tq, d), tile_map),
                  pl.BlockSpec((tq, LANES), tile_map),
                  pl.BlockSpec((seq, d), batch_map, pipeline_mode=pl.Buffered(1)),
                  pl.BlockSpec((None, N_HEADS * V_ROWS, seq), lambda b, j: (b, 0, 0), pipeline_mode=pl.Buffered(1)),
                  pl.BlockSpec((seq, IDX_DIM), batch_map, pipeline_mode=pl.Buffered(1))]
                 + [pl.BlockSpec((tq, d), tile_map)] * len(stacked),
        out_specs=[pl.BlockSpec((tq, d), tile_map)] + [stack_spec] * n_arrays,
        out_shape=[jax.ShapeDtypeStruct((n, d), BF16)] + [jax.ShapeDtypeStruct((n_layers, n, d), F32)] * n_arrays,
        scratch_shapes=[pltpu.VMEM((seq, tq), F32), pltpu.VMEM((LANES, tq), F32),
                        pltpu.VMEM((N_HEADS * V_ROWS, tq), F32), pltpu.VMEM((2, N_HEADS, kc, tq), F32)],
        compiler_params=_compiler_params(("parallel", "arbitrary")),
        name="attn_prompt",
    )(qb, qib, wi, kb, vt, kib, *stacked)
    return outs if stacked else outs[0]


def _sample_scores_kernel(pt_ref, qi_ref, w_ref, kin_ref, *rest, pages_per_step):
    del pt_ref
    page_refs = rest[:pages_per_step]
    score_ref, self_ref = rest[pages_per_step:]
    qi = qi_ref[...]
    w = w_ref[...]

    rows = []
    for r in range(pages_per_step):
        d = _dot_nt(qi, page_refs[r][...].astype(BF16))
        rows.append(jnp.sum(w * jnp.maximum(d, 0.0), axis=0, keepdims=True))
    score_ref[...] = jnp.concatenate(rows, axis=0)

    @pl.when(pl.program_id(1) == 0)
    def _():
        d_self = jnp.sum(qi.astype(F32) * kin_ref[...].astype(BF16).astype(F32), axis=1, keepdims=True)
        self_ref[...] = jnp.sum(w * jnp.maximum(d_self, 0.0), axis=0, keepdims=True)


def _sample_scores(qi8, w8, ki_new, cache_kidx, page_table, layer):
    db, n_pages = page_table.shape
    pps = min(IDX_PAGES_PER_STEP, n_pages)
    steps = n_pages // pps

    def page_spec(r):
        return pl.BlockSpec((None, None, PAGE_SIZE, IDX_DIM), lambda b, g, pt: (layer, pt[b, g * pps + r], 0, 0))

    per_seq = lambda b, g, pt: (b, 0, 0)
    grid_spec = pltpu.PrefetchScalarGridSpec(
        num_scalar_prefetch=1,
        grid=(db, steps),
        in_specs=[pl.BlockSpec((None, IDX_HEADS, IDX_DIM), per_seq), pl.BlockSpec((None, IDX_HEADS, LANES), per_seq),
                  pl.BlockSpec((None, 1, IDX_DIM), per_seq)] + [page_spec(r) for r in range(pps)],
        out_specs=[pl.BlockSpec((None, pps, PAGE_SIZE), lambda b, g, pt: (b, g, 0)),
                   pl.BlockSpec((None, 1, LANES), per_seq)],
    )
    return pl.pallas_call(
        functools.partial(_sample_scores_kernel, pages_per_step=pps),
        grid_spec=grid_spec,
        out_shape=(jax.ShapeDtypeStruct((db, n_pages, PAGE_SIZE), F32), jax.ShapeDtypeStruct((db, 1, LANES), F32)),
        compiler_params=_compiler_params(("parallel", "arbitrary")),
        name="sample_scores",
    )(page_table, qi8, w8, ki_new, *([cache_kidx] * pps))


def _sample_select_kernel(x_ref, xself_ref, sel_ref, selself_ref, *, topk):
    x = x_ref[...]
    x_self = xself_ref[...][:, :, :1]
    db, n_pages, ps = x.shape

    def total(v, op):
        return op(op(v, axis=1, keepdims=True), axis=2, keepdims=True)

    def count_ge(p):
        return total(jnp.where(x >= p, 1.0, 0.0), jnp.sum) + jnp.where(x_self >= p, 1.0, 0.0)

    def max_below(hi):
        return jnp.maximum(total(jnp.where(x < hi, x, -jnp.inf), jnp.max), jnp.where(x_self < hi, x_self, -jnp.inf))

    row_max = jnp.maximum(total(x, jnp.max), x_self)
    row_min = jnp.minimum(total(x, jnp.min), x_self)
    n_valid = jnp.full((db, 1, 1), float(n_pages * ps + 1), F32)
    kk = jnp.full((db, 1, 1), float(topk), F32)
    lo, hi, clo, chi = _topk_bracket(count_ge, max_below, row_min, row_max, n_valid, kk)
    has_ties = _any(clo != kk) > 0

    @pl.when(jnp.logical_not(has_ties))
    def _():
        sel_ref[...] = jnp.where(x >= lo, 1.0, 0.0)
        selself_ref[...] = jnp.broadcast_to(jnp.where(x_self >= lo, 1.0, 0.0), selself_ref.shape)

    @pl.when(has_ties)
    def _():
        need = kk - chi
        cand = jnp.where((x >= lo) & (x < hi), 1.0, 0.0)
        upto = jnp.where(lax.broadcasted_iota(jnp.int32, (ps, ps), 0) <= lax.broadcasted_iota(jnp.int32, (ps, ps), 1),
                         1.0, 0.0).astype(BF16)
        below = jnp.where(lax.broadcasted_iota(jnp.int32, (n_pages, n_pages), 1)
                          < lax.broadcasted_iota(jnp.int32, (n_pages, n_pages), 0), 1.0, 0.0).astype(BF16)
        in_page = _dot(cand.reshape(db * n_pages, ps).astype(BF16), upto).reshape(db, n_pages, ps)
        page_tot = jnp.broadcast_to(jnp.sum(cand, axis=2, keepdims=True), cand.shape).astype(BF16)
        for b in range(db):
            rank = in_page[b] + _dot(below, page_tot[b])
            take = (cand[b] > 0.0) & (rank <= need[b])
            sel_ref[b] = jnp.where((x[b] >= hi[b]) | take, 1.0, 0.0)
        cand_self = (x_self >= lo) & (x_self < hi)
        take_self = cand_self & (total(cand, jnp.sum) + 1.0 <= need)
        selself_ref[...] = jnp.broadcast_to(jnp.where((x_self >= hi) | take_self, 1.0, 0.0), selself_ref.shape)


def _sample_select(scores, score_self, topk):
    args = (scores, score_self)
    return pl.pallas_call(
        functools.partial(_sample_select_kernel, topk=topk),
        grid=(1,),
        in_specs=[_resident(a.shape) for a in args],
        out_specs=[pl.BlockSpec(a.shape, lambda i: (0, 0, 0)) for a in args],
        out_shape=tuple(jax.ShapeDtypeStruct(a.shape, F32) for a in args),
        compiler_params=_compiler_params(("arbitrary",)),
        name="sample_select",
    )(*args)


def _sample_attn_kernel(pt_ref, q_ref, sel_ref, selself_ref, kn_ref, vn_ref, spread_ref, *rest, pages_per_step):
    del pt_ref
    k_refs = rest[:pages_per_step]
    v_refs = rest[pages_per_step:2 * pages_per_step]
    o_ref, m_ref, l_ref, acc_ref = rest[2 * pages_per_step:]
    g = pl.program_id(1)
    q = q_ref[...]
    q_bf = q.astype(BF16)
    cols = PAGE_SIZE * N_HEADS
    own = (lax.broadcasted_iota(jnp.int32, (N_HEADS, cols), 1) % N_HEADS
           == lax.broadcasted_iota(jnp.int32, (N_HEADS, cols), 0))

    @pl.when(g == 0)
    def _():
        m_ref[...] = jnp.full(m_ref.shape, -jnp.inf, F32)
        l_ref[...] = jnp.zeros(l_ref.shape, F32)
        acc_ref[...] = jnp.zeros(acc_ref.shape, F32)

    def accumulate(logits, value_fns):
        m_old = m_ref[...]
        m_new = m_old
        for s in logits:
            m_new = jnp.maximum(m_new, jnp.max(s, axis=1, keepdims=True))
        m_safe = jnp.where(m_new == -jnp.inf, 0.0, m_new)
        alpha = jnp.exp2(m_old - m_safe)
        l = alpha * l_ref[...]
        acc = alpha * acc_ref[...]
        for s, value_fn in zip(logits, value_fns):
            p = jnp.exp2(s - m_safe)
            l = l + jnp.sum(p, axis=1, keepdims=True)
            acc = acc + value_fn(p)
        l_ref[...] = l
        acc_ref[...] = acc
        m_ref[...] = m_new

    sel_cols = _dot(sel_ref[...].astype(BF16), spread_ref[...])
    logits = []
    for r in range(pages_per_step):
        s = _dot_nt(q_bf, k_refs[r][...].astype(BF16))
        logits.append(jnp.where(own & (sel_cols[r:r + 1, :] > 0.0), s, -jnp.inf))
    accumulate(logits, [lambda p, r=r: _dot(p.astype(BF16), v_refs[r][...].astype(BF16))
                        for r in range(pages_per_step)])

    @pl.when(g == pl.num_programs(1) - 1)
    def _():
        kn = kn_ref[...].astype(BF16).astype(F32)
        s_self = jnp.sum(q * kn, axis=1, keepdims=True)
        s_self = jnp.where(selself_ref[:, :1] > 0.0, s_self, -jnp.inf)
        vn = vn_ref[...].astype(BF16).astype(F32)
        accumulate([s_self], [lambda p: p.astype(BF16).astype(F32) * vn])
        o_ref[...] = acc_ref[...] / l_ref[...]


def _sample_attn(q, sel, sel_self, k_new, v_new, cache_k, cache_v, page_table, layer):
    db, n_pages = page_table.shape
    pps = min(KV_PAGES_PER_STEP, n_pages)
    steps = n_pages // pps
    cols = PAGE_SIZE * N_HEADS
    sel4 = sel.reshape(db, steps, pps, PAGE_SIZE)
    spread = (jnp.arange(cols)[None, :] // N_HEADS == jnp.arange(PAGE_SIZE)[:, None]).astype(BF16)

    def page_spec(r):
        return pl.BlockSpec((None, None, cols, HEAD_DIM), lambda b, g, pt: (layer, pt[b, g * pps + r], 0, 0))

    per_seq = lambda b, g, pt: (b, 0, 0)
    heads_spec = pl.BlockSpec((None, N_HEADS, HEAD_DIM), per_seq)
    grid_spec = pltpu.PrefetchScalarGridSpec(
        num_scalar_prefetch=1,
        grid=(db, steps),
        in_specs=[heads_spec,
                  pl.BlockSpec((None, None, pps, PAGE_SIZE), lambda b, g, pt: (b, g, 0, 0)),
                  pl.BlockSpec((None, 1, LANES), per_seq), heads_spec, heads_spec,
                  pl.BlockSpec((PAGE_SIZE, cols), lambda b, g, pt: (0, 0))]
                 + [page_spec(r) for r in range(pps)] * 2,
        out_specs=heads_spec,
        scratch_shapes=[pltpu.VMEM((N_HEADS, 1), F32), pltpu.VMEM((N_HEADS, 1), F32),
                        pltpu.VMEM((N_HEADS, HEAD_DIM), F32)],
    )
    return pl.pallas_call(
        functools.partial(_sample_attn_kernel, pages_per_step=pps),
        grid_spec=grid_spec,
        out_shape=jax.ShapeDtypeStruct((db, N_HEADS, HEAD_DIM), F32),
        compiler_params=_compiler_params(("parallel", "arbitrary")),
        name="sample_attn",
    )(page_table, q, sel4, sel_self, k_new, v_new, spread, *([cache_k] * pps), *([cache_v] * pps))


def _ln_swish(y, g, b):
    yc = y - jnp.mean(y, axis=-1, keepdims=True)
    yn = yc * lax.rsqrt(jnp.mean(yc * yc, axis=-1, keepdims=True) + EPS) * g + b
    return yn * jax.nn.sigmoid(yn)


def _conv_prompt_kernel(cur_ref, halo_ref, w_ref, bdw_ref, lng_ref, lnb_ref, y_ref, sh_ref, conv_ref):
    tt, d = cur_ref.shape
    sh_ref[0, :CONV_HALO, :] = jnp.where(pl.program_id(1) == 0, 0.0, halo_ref[...])
    sh_ref[0, CONV_HALO:, :] = cur_ref[...]
    n_sh = tt + CONV_HALO - SUBLANES
    for r in range(1, SUBLANES):
        sh_ref[r, :n_sh, :] = sh_ref[0, r:r + n_sh, :]
    first = CONV_HALO - CONV_CTX

    def block(rb, carry):
        r0 = pl.multiple_of(rb * CONV_ROW_BLOCK, CONV_ROW_BLOCK)
        for lt in range(d // LANES):
            ls = slice(lt * LANES, (lt + 1) * LANES)
            acc = jnp.zeros((CONV_ROW_BLOCK, LANES), F32)
            for r in range(SUBLANES):
                taps = [j for j in range(CONV_WIDTH) if (j + first) % SUBLANES == r]
                a_max = max((j + first) // SUBLANES for j in taps)
                rows = sh_ref[r, pl.ds(r0, a_max * SUBLANES + CONV_ROW_BLOCK), ls]
                for j in taps:
                    a = (j + first) // SUBLANES
                    acc = acc + w_ref[j:j + 1, ls] * rows[a * SUBLANES:a * SUBLANES + CONV_ROW_BLOCK]
            conv_ref[pl.ds(r0, CONV_ROW_BLOCK), ls] = acc
        return carry

    lax.fori_loop(0, tt // CONV_ROW_BLOCK, block, 0)
    y = conv_ref[...] + bdw_ref[...]
    y_ref[...] = _ln_swish(y, lng_ref[...], lnb_ref[...]).astype(BF16)


def _conv_prompt(glu, w_dw, b_dw, ln_g, ln_b, batch, seq):
    n, d = glu.shape
    tt = min(CONV_T_TILE, seq)
    nt = seq // tt
    halo_per_tile = tt // CONV_HALO
    halo_per_seq = seq // CONV_HALO
    tile_map = lambda b, j: (b * nt + j, 0)
    halo_map = lambda b, j: (jnp.maximum(b * halo_per_seq + j * halo_per_tile - 1, 0), 0)
    consts = (w_dw, b_dw, ln_g, ln_b)
    return pl.pallas_call(
        _conv_prompt_kernel,
        grid=(batch, nt),
        in_specs=[pl.BlockSpec((tt, d), tile_map), pl.BlockSpec((CONV_HALO, d), halo_map)]
                 + [_resident(c.shape) for c in consts],
        out_specs=pl.BlockSpec((tt, d), tile_map),
        out_shape=jax.ShapeDtypeStruct((n, d), BF16),
        scratch_shapes=[pltpu.VMEM((SUBLANES, tt + CONV_HALO, d), F32), pltpu.VMEM((tt, d), F32)],
        compiler_params=_compiler_params(("parallel", "arbitrary")),
        name="conv_prompt",
    )(glu, glu, *consts)


def _conv_sample_kernel(state_ref, u_ref, w_ref, bdw_ref, lng_ref, lnb_ref, y_ref, conv_ref):
    db = u_ref.shape[0]
    w_ctx = w_ref[:CONV_CTX, :]
    for b in range(db):
        conv_ref[b:b + 1, :] = jnp.sum(state_ref[b] * w_ctx, axis=0, keepdims=True)
    y = conv_ref[...] + u_ref[...] * w_ref[CONV_CTX:CONV_CTX + 1, :] + bdw_ref[...]
    y_ref[...] = _ln_swish(y, lng_ref[...], lnb_ref[...]).astype(BF16)


def _conv_sample(state_all, layer, u, w_dw, b_dw, ln_g, ln_b):
    db, d = u.shape
    args = (state_all, u, w_dw, b_dw, ln_g, ln_b)
    state_spec = pl.BlockSpec((None,) + state_all.shape[1:], lambda i: (layer, 0, 0, 0),
                              pipeline_mode=pl.Buffered(1))
    return pl.pallas_call(
        _conv_sample_kernel,
        grid=(1,),
        in_specs=[state_spec] + [_resident(a.shape) for a in args[1:]],
        out_specs=pl.BlockSpec((db, d), lambda i: (0, 0)),
        out_shape=jax.ShapeDtypeStruct((db, d), BF16),
        scratch_shapes=[pltpu.VMEM((db, d), F32)],
        compiler_params=_compiler_params(("arbitrary",)),
        name="conv_sample",
    )(*args)


def kernel(x_prompt, x_sample, p_prompt, p_sample, cache_k, cache_v, cache_kidx, state_conv, page_table, ffn1_norm, ffn1_w_gate, ffn1_w_up, ffn1_w_down, mix_norm, attn_w_in, attn_q_gain, attn_k_gain, attn_kidx_gain, attn_w_o, conv_w_in, conv_b_in, conv_w_dw, conv_b_dw, conv_ln_g, conv_ln_b, conv_w_out, conv_b_out, ffn2_norm, ffn2_w_gate, ffn2_w_up, ffn2_w_down, ple_norm, ple_w_gate, ple_w_proj):
    batch, seq, d = x_prompt.shape
    db, t_new, _ = x_sample.shape
    depth = ffn1_norm.shape[0]
    n_mixers = 2
    n_pages = page_table.shape[1]
    past = n_pages * PAGE_SIZE
    attn_dim = N_HEADS * HEAD_DIM
    assert t_new == 1 and d == attn_dim
    topk_p = min(TOPK_MAX, seq // 4)
    topk_s = min(TOPK_MAX, (past + t_new) // 4)
    n_p = batch * seq
    tile_p = min(ROW_TILE, seq)
    tile_s = db

    row = lambda v: v.reshape(1, -1)
    bf = lambda w: w.astype(BF16)
    hp = x_prompt.reshape(n_p, d)
    hs = x_sample.reshape(db, d)
    cache_k4 = cache_k.reshape(cache_k.shape[0], cache_k.shape[1], PAGE_SIZE * N_HEADS, HEAD_DIM)
    cache_v4 = cache_v.reshape(cache_v.shape[0], cache_v.shape[1], PAGE_SIZE * N_HEADS, HEAD_DIM)
    zero_bias = jnp.zeros((1, d), F32)
    n_attn = attn_w_in.shape[0]
    pp_all = p_prompt.reshape(depth, n_p, -1)
    ps_all = p_sample.reshape(depth, db, -1)

    k_p, v_p, ki_p, conv_p = [], [], [], []
    k_s, v_s, ki_s, conv_s = [], [], [], []
    for i in range(depth):
        ffn1 = (row(ffn1_norm[i]), bf(ffn1_w_gate[i]), bf(ffn1_w_up[i]), bf(ffn1_w_down[i]))
        ffn2 = (row(ffn2_norm[i]), bf(ffn2_w_gate[i]), bf(ffn2_w_up[i]), bf(ffn2_w_down[i]))
        ple = (row(ple_norm[i]), bf(ple_w_gate[i]), bf(ple_w_proj[i]))
        g_mix = row(mix_norm[i])
        if i % n_mixers == 0:
            a = i // n_mixers
            w_in = bf(attn_w_in[a])
            o_v, o_wi = 2 * attn_dim, 3 * attn_dim + IDX_HEADS * IDX_DIM + IDX_DIM
            w_in_t = w_in.T
            w_wi = jnp.pad(w_in_t[o_wi:], ((0, LANES - IDX_HEADS), (0, 0))).T
            proj = (w_in, w_in_t[o_v:o_v + attn_dim], w_wi, row(attn_q_gain[a]), row(attn_k_gain[a]),
                    row(attn_kidx_gain[a]))
            w_out, b_out = bf(attn_w_o[a]), zero_bias

            hp, qb, kf, vf, kif, kb, vt, qib, kib, wi = _stage_a_attn(hp, ffn1, g_mix, proj, tile_p, batch)
            k_p.append(kf)
            v_p.append(vf)
            ki_p.append(kif.reshape(batch, seq, IDX_DIM))
            if n_attn == 2 and a == n_attn - 1:
                mp, k_stack, v_stack = _attn_prompt(qb, qib, wi, kb, vt, kib, batch, seq, topk_p,
                                                    stacked=(k_p[0], v_p[0], k_p[1], v_p[1]))
            else:
                mp = _attn_prompt(qb, qib, wi, kb, vt, kib, batch, seq, topk_p)

            hs, qb, kf, vf, kif, kb, vt, qib, kib, wi = _stage_a_attn(hs, ffn1, g_mix, proj, tile_s, 1)
            qi8 = qib.reshape(db, IDX_HEADS, IDX_DIM)
            w8 = jnp.broadcast_to(wi[:, :IDX_HEADS, None], (db, IDX_HEADS, LANES))
            scores, score_self = _sample_scores(qi8, w8, kif.reshape(db, 1, IDX_DIM), cache_kidx, page_table, a)
            sel, sel_self = _sample_select(scores, score_self, topk_s)
            per_head = lambda v: v.astype(F32).reshape(db, N_HEADS, HEAD_DIM)
            ms = _sample_attn(per_head(qb), sel, sel_self, per_head(kf), per_head(vf), cache_k4, cache_v4,
                              page_table, a)
            ms = ms.reshape(db, d).astype(BF16)
            k_s.append(kf.reshape(db, t_new, N_HEADS, HEAD_DIM))
            v_s.append(vf.reshape(db, t_new, N_HEADS, HEAD_DIM))
            ki_s.append(kif.reshape(db, t_new, IDX_DIM))
        else:
            c = i // n_mixers
            w_dw = jnp.pad(conv_w_dw[c], ((0, CONV_HALO - CONV_WIDTH), (0, 0)))
            conv = (w_dw, row(conv_b_dw[c]), row(conv_ln_g[c]), row(conv_ln_b[c]))
            w_out, b_out = bf(conv_w_out[c]), row(conv_b_out[c])

            hp, glu = _stage_a_conv(hp, ffn1, g_mix, bf(conv_w_in[c]), row(conv_b_in[c]), tile_p)
            mp = _conv_prompt(glu, *conv, batch, seq)
            conv_p.append(glu.reshape(batch, seq, d)[:, seq - CONV_CTX:])

            hs, glu = _stage_a_conv(hs, ffn1, g_mix, bf(conv_w_in[c]), row(conv_b_in[c]), tile_s)
            ms = _conv_sample(state_conv, c, glu, *conv)
            conv_s.append(jnp.concatenate([state_conv[c], glu[:, None, :]], axis=1)[:, -CONV_CTX:])

        hp = _stage_c(hp, mp, pp_all, i, w_out, b_out, ffn2, ple, tile_p)
        hs = _stage_c(hs, ms, ps_all, i, w_out, b_out, ffn2, ple, tile_s)

    if n_attn != 2:
        k_stack, v_stack = jnp.stack(k_p), jnp.stack(v_p)
    kv_shape = (n_attn, batch, seq, N_HEADS, HEAD_DIM)
    return (hp.reshape(batch, seq, d), hs.reshape(db, t_new, d), k_stack.reshape(kv_shape), v_stack.reshape(kv_shape),
            jnp.stack(ki_p),
            jnp.stack(conv_p), jnp.stack(k_s), jnp.stack(v_s), jnp.stack(ki_s), jnp.stack(conv_s))
```

```python
import functools

import jax
import jax.numpy as jnp
from jax import lax
from jax.experimental import pallas as pl
from jax.experimental.pallas import tpu as pltpu

F32 = jnp.float32
BF16 = jnp.bfloat16

N_HEADS = 8
HEAD_DIM = 128
IDX_HEADS = 8
IDX_DIM = 128
TOPK_MAX = 256
PAGE_SIZE = 128
CONV_WIDTH = 31
CONV_CTX = CONV_WIDTH - 1
EPS = 1e-6

LANES = 128
SUBLANES = 8
VMEM_LIMIT_BYTES = 58 * 1024 * 1024
ROW_TILE = 512
FF_CHUNK = 256
ATTN_Q_TILE = 256
ATTN_K_CHUNK = 512
CONV_T_TILE = 256
CONV_HALO = 32
CONV_ROW_BLOCK = 32
N_BISECT = 15
FOLD_ROWS = 32
LOG2_E = 1.4426950408889634
V_ROWS = HEAD_DIM + 16
BISECT_STEPS_PER_CHECK = 3
IDX_PAGES_PER_STEP = 32
KV_PAGES_PER_STEP = 16

_NT = (((1,), (1,)), ((), ()))


def _compiler_params(semantics):
    return pltpu.CompilerParams(dimension_semantics=semantics, vmem_limit_bytes=VMEM_LIMIT_BYTES)


def _resident(shape):
    nd = len(shape)
    return pl.BlockSpec(shape, lambda *_: (0,) * nd, pipeline_mode=pl.Buffered(1))


def _rows(tile, cols):
    return pl.BlockSpec((tile, cols), lambda i: (i, 0))


def _rms(x, g):
    return x * lax.rsqrt(jnp.mean(x * x, axis=-1, keepdims=True) + EPS) * g


def _dot(a, b):
    return jnp.dot(a, b, preferred_element_type=F32)


def _dot_nt(a, b):
    return lax.dot_general(a, b, _NT, preferred_element_type=F32)


def _ffn_half_step(x, g_ref, wg_ref, wu_ref, wd_ref, act_ref):
    hn = _rms(x, g_ref[...]).astype(BF16)
    d_ff = wg_ref.shape[1]
    for c in range(d_ff // FF_CHUNK):
        sl = slice(c * FF_CHUNK, (c + 1) * FF_CHUNK)
        g = _dot(hn, wg_ref[:, sl])
        u = _dot(hn, wu_ref[:, sl])
        act_ref[:, sl] = (g * jax.nn.sigmoid(g) * u).astype(BF16)
    return x + 0.5 * _dot(act_ref[...], wd_ref[...])


def _stage_a_attn_kernel(x_ref, g1_ref, wg_ref, wu_ref, wd_ref, gm_ref, win_ref, wwi_ref, qg_ref, kg_ref, kig_ref,
                         h_ref, qb_ref, kf_ref, vf_ref, kif_ref, kb_ref, vt_ref, qib_ref, kib_ref, wi_ref,
                         act_ref):
    h = _ffn_half_step(x_ref[...], g1_ref, wg_ref, wu_ref, wd_ref, act_ref)
    h_ref[...] = h
    hn = _rms(h, gm_ref[...]).astype(BF16)

    attn_dim = N_HEADS * HEAD_DIM
    o_qi = 3 * attn_dim
    o_ki = o_qi + IDX_HEADS * IDX_DIM
    wq_ref = win_ref.at[:, 0:attn_dim]
    wk_ref = win_ref.at[:, attn_dim:2 * attn_dim]
    wv_ref = win_ref.at[:, 2 * attn_dim:o_qi]
    wqi_ref = win_ref.at[:, o_qi:o_ki]
    wki_ref = win_ref.at[:, o_ki:o_ki + IDX_DIM]

    zq = _dot(hn, wq_ref[...])
    for hh in range(N_HEADS):
        sl = slice(hh * HEAD_DIM, (hh + 1) * HEAD_DIM)
        qb_ref[:, sl] = (_rms(zq[:, sl], qg_ref[...]) * (HEAD_DIM ** -0.5 * LOG2_E)).astype(BF16)
    zk = _dot(hn, wk_ref[...])
    for hh in range(N_HEADS):
        sl = slice(hh * HEAD_DIM, (hh + 1) * HEAD_DIM)
        kn = _rms(zk[:, sl], kg_ref[...])
        kf_ref[:, sl] = kn
        kb_ref[:, sl] = kn.astype(BF16)
    zv = _dot(hn, wv_ref[...])
    vf_ref[...] = zv
    vt = zv.T.astype(BF16)
    for hh in range(N_HEADS):
        vt_ref[hh * V_ROWS:hh * V_ROWS + HEAD_DIM, :] = vt[hh * HEAD_DIM:(hh + 1) * HEAD_DIM, :]
        vt_ref[hh * V_ROWS + HEAD_DIM:(hh + 1) * V_ROWS, :] = jnp.ones((V_ROWS - HEAD_DIM, vt.shape[1]), BF16)
    qib_ref[...] = _dot(hn, wqi_ref[...]).astype(BF16)
    kin = _rms(_dot(hn, wki_ref[...]), kig_ref[...])
    kif_ref[...] = kin
    kib_ref[...] = kin.astype(BF16)
    wi_ref[...] = _dot(hn, wwi_ref[...])


def _stage_a_attn(x, ffn, g_mix, proj, tile, batch):
    n, d = x.shape
    g1, wg, wu, wd = ffn
    w_in, wwi, qg, kg, kig = proj
    d_ff = wg.shape[1]
    consts = (g1, wg, wu, wd, g_mix, w_in, wwi, qg, kg, kig)
    sds = jax.ShapeDtypeStruct
    seq = n // batch
    tiles_per_seq = seq // tile
    out_shape = (sds((n, d), F32), sds((n, d), BF16), sds((n, d), F32), sds((n, d), F32),
                 sds((n, IDX_DIM), F32), sds((n, d), BF16), sds((batch, N_HEADS * V_ROWS, seq), BF16), sds((n, d), BF16),
                 sds((n, IDX_DIM), BF16), sds((n, LANES), F32))
    out_specs = [_rows(tile, s.shape[1]) for s in out_shape]
    out_specs[6] = pl.BlockSpec((None, N_HEADS * V_ROWS, tile),
                                lambda i: (i // tiles_per_seq, 0, i % tiles_per_seq))
    return pl.pallas_call(
        _stage_a_attn_kernel,
        grid=(n // tile,),
        in_specs=[_rows(tile, d)] + [_resident(c.shape) for c in consts],
        out_specs=out_specs,
        out_shape=out_shape,
        scratch_shapes=[pltpu.VMEM((tile, d_ff), BF16)],
        compiler_params=_compiler_params(("parallel",)),
        name="stage_a_attn",
    )(x, *consts)


def _stage_a_conv_kernel(x_ref, g1_ref, wg_ref, wu_ref, wd_ref, gm_ref, wci_ref, bci_ref,
                         h_ref, glu_ref, act_ref):
    h = _ffn_half_step(x_ref[...], g1_ref, wg_ref, wu_ref, wd_ref, act_ref)
    h_ref[...] = h
    hn = _rms(h, gm_ref[...]).astype(BF16)
    d = h.shape[1]
    u = _dot(hn, wci_ref[...]) + bci_ref[...]
    glu_ref[...] = u[:, :d] * jax.nn.sigmoid(u[:, d:])


def _stage_a_conv(x, ffn, g_mix, wci, bci, tile):
    n, d = x.shape
    g1, wg, wu, wd = ffn
    consts = (g1, wg, wu, wd, g_mix, wci, bci)
    out_shape = (jax.ShapeDtypeStruct((n, d), F32), jax.ShapeDtypeStruct((n, d), F32))
    return pl.pallas_call(
        _stage_a_conv_kernel,
        grid=(n // tile,),
        in_specs=[_rows(tile, d)] + [_resident(c.shape) for c in consts],
        out_specs=[_rows(tile, d), _rows(tile, d)],
        out_shape=out_shape,
        scratch_shapes=[pltpu.VMEM((tile, wg.shape[1]), BF16)],
        compiler_params=_compiler_params(("parallel",)),
        name="stage_a_conv",
    )(x, *consts)


def _stage_c_kernel(h_ref, m_ref, p_ref, wo_ref, bo_ref, g2_ref, wg_ref, wu_ref, wd_ref, gp_ref, wpg_ref,
                    wpp_ref, out_ref, act_ref):
    h = h_ref[...] + _dot(m_ref[...], wo_ref[...]) + bo_ref[...]
    h = _ffn_half_step(h, g2_ref, wg_ref, wu_ref, wd_ref, act_ref)
    gate = jax.nn.sigmoid(_dot(_rms(h, gp_ref[...]).astype(BF16), wpg_ref[...]))
    out_ref[...] = h + gate * _dot(p_ref[...].astype(BF16), wpp_ref[...])


def _stage_c(h, m, p_all, layer, wo, bo, ffn, ple, tile):
    n, d = h.shape
    g2, wg, wu, wd = ffn
    gp, wpg, wpp = ple
    consts = (wo, bo, g2, wg, wu, wd, gp, wpg, wpp)
    p_spec = pl.BlockSpec((None, tile, p_all.shape[2]), lambda i: (layer, i, 0))
    return pl.pallas_call(
        _stage_c_kernel,
        grid=(n // tile,),
        in_specs=[_rows(tile, d), _rows(tile, d), p_spec] + [_resident(c.shape) for c in consts],
        out_specs=_rows(tile, d),
        out_shape=jax.ShapeDtypeStruct((n, d), F32),
        scratch_shapes=[pltpu.VMEM((tile, wg.shape[1]), BF16)],
        compiler_params=_compiler_params(("parallel",)),
        name="stage_c",
    )(h, m, p_all, *consts)


def _any(flag):
    return jnp.max(jnp.where(flag, 1.0, 0.0)).astype(jnp.int32)


def _topk_bracket(count_ge, max_below, row_min, row_max, n_valid, kk):
    def update(lo, hi, clo, chi, p, c):
        pend = clo != kk
        up = pend & (c >= kk)
        dn = pend & (c < kk)
        return (jnp.where(up, p, lo), jnp.where(dn, p, hi), jnp.where(up, c, clo), jnp.where(dn, c, chi))

    state = update(row_min, jnp.full_like(row_min, jnp.inf), n_valid, jnp.zeros_like(row_min),
                   row_max, count_ge(row_max))

    def bisect_cond(carry):
        return (carry[0] < N_BISECT) & (carry[1] > 0)

    def bisect_body(carry):
        it, _, *st = carry
        for _ in range(BISECT_STEPS_PER_CHECK):
            p = 0.5 * st[0] + 0.5 * st[1]
            st = update(*st, p, count_ge(p))
        return (it + BISECT_STEPS_PER_CHECK, _any(st[2] != kk)) + tuple(st)

    carry = lax.while_loop(bisect_cond, bisect_body, (jnp.int32(0), _any(state[2] != kk)) + state)
    lo, hi, clo, chi = carry[2:]
    tied = jnp.zeros_like(lo)

    def peel_cond(carry):
        return carry[0] > 0

    def peel_body(carry):
        _, lo, hi, clo, chi, tied = carry
        m = max_below(hi)
        c = count_ge(m)
        pend = (clo != kk) & (tied == 0.0)
        fin = pend & (c >= kk)
        dn = pend & (c < kk)
        lo = jnp.where(fin, m, lo)
        clo = jnp.where(fin, c, clo)
        tied = jnp.where(fin, 1.0, tied)
        hi = jnp.where(dn, m, hi)
        chi = jnp.where(dn, c, chi)
        return (_any((clo != kk) & (tied == 0.0)), lo, hi, clo, chi, tied)

    carry = lax.while_loop(peel_cond, peel_body, (_any(clo != kk), lo, hi, clo, chi, tied))
    return carry[1:5]


def _fold_rows(x, op):
    return op(x.reshape(x.shape[0] // FOLD_ROWS, FOLD_ROWS, x.shape[1]), axis=0)


def _attn_prompt_kernel(q_ref, qi_ref, wi_ref, k_ref, vt_ref, ki_ref, *rest, topk, n_stacked):
    stack_in = rest[:n_stacked]
    o_ref = rest[n_stacked]
    stack_out = rest[n_stacked + 1:n_stacked + 1 + n_stacked // 2]
    score_ref, wt_ref, acc_ref, s_ref = rest[n_stacked + 1 + n_stacked // 2:]
    for i, src_ref in enumerate(stack_in):
        stack_out[i % (n_stacked // 2)][i // (n_stacked // 2)] = src_ref[...]
    tq = q_ref.shape[0]
    seq = k_ref.shape[0]
    kc = min(ATTN_K_CHUNK, seq)
    q0 = pl.program_id(1) * tq
    n_ch = (q0 + tq + kc - 1) // kc
    q_pos = q0 + lax.broadcasted_iota(jnp.int32, (1, tq), 1)

    def chunk(c):
        return pl.ds(pl.multiple_of(c * kc, kc), kc)

    wt_ref[...] = wi_ref[...].T

    def score_body(c, carry):
        mx, mn = carry
        kic = ki_ref[chunk(c), :]
        acc = jnp.zeros((kc, tq), F32)
        for hh in range(IDX_HEADS):
            d = _dot_nt(kic, qi_ref[:, hh * IDX_DIM:(hh + 1) * IDX_DIM])
            acc = acc + wt_ref[hh:hh + 1, :] * jnp.maximum(d, 0.0)
        key_pos = c * kc + lax.broadcasted_iota(jnp.int32, (kc, 1), 0)
        allowed = key_pos <= q_pos
        sc = jnp.where(allowed, acc, -jnp.inf)
        score_ref[chunk(c), :] = sc
        mx = jnp.maximum(mx, _fold_rows(sc, jnp.max))
        mn = jnp.minimum(mn, _fold_rows(jnp.where(allowed, acc, jnp.inf), jnp.min))
        return mx, mn

    mx, mn = lax.fori_loop(0, n_ch, score_body,
                           (jnp.full((FOLD_ROWS, tq), -jnp.inf, F32), jnp.full((FOLD_ROWS, tq), jnp.inf, F32)))
    row_max = jnp.max(mx, axis=0, keepdims=True)
    row_min = jnp.min(mn, axis=0, keepdims=True)

    def count_ge(p):
        def body(c, acc):
            return acc + _fold_rows(jnp.where(score_ref[chunk(c), :] >= p, 1.0, 0.0), jnp.sum)
        acc = lax.fori_loop(0, n_ch, body, jnp.zeros((FOLD_ROWS, tq), F32))
        return jnp.sum(acc, axis=0, keepdims=True)

    def max_below(hi):
        def body(c, acc):
            x = score_ref[chunk(c), :]
            return jnp.maximum(acc, _fold_rows(jnp.where(x < hi, x, -jnp.inf), jnp.max))
        acc = lax.fori_loop(0, n_ch, body, jnp.full((FOLD_ROWS, tq), -jnp.inf, F32))
        return jnp.max(acc, axis=0, keepdims=True)

    n_valid = (q_pos + 1).astype(F32)
    kk = jnp.minimum(n_valid, float(topk))
    lo, hi, clo, chi = _topk_bracket(count_ge, max_below, row_min, row_max, n_valid, kk)
    has_ties = _any(clo != kk) > 0

    @pl.when(jnp.logical_not(has_ties))
    def _():
        def body(c, carry):
            score_ref[chunk(c), :] = jnp.where(score_ref[chunk(c), :] >= lo, 0.0, -jnp.inf)
            return carry
        lax.fori_loop(0, n_ch, body, 0)

    @pl.when(has_ties)
    def _():
        need = kk - chi
        upto = jnp.where(lax.broadcasted_iota(jnp.int32, (kc, kc), 1) <= lax.broadcasted_iota(jnp.int32, (kc, kc), 0),
                         1.0, 0.0).astype(BF16)

        def body(c, before):
            x = score_ref[chunk(c), :]
            cand = jnp.where((x >= lo) & (x < hi), 1.0, 0.0)
            rank = before + _dot(upto, cand.astype(BF16))
            take = (cand > 0.0) & (rank <= need)
            score_ref[chunk(c), :] = jnp.where((x >= hi) | take, 0.0, -jnp.inf)
            return before + jnp.sum(cand, axis=0, keepdims=True)
        lax.fori_loop(0, n_ch, body, jnp.zeros((1, tq), F32))

    acc_ref[...] = jnp.zeros(acc_ref.shape, F32)

    def logits(c, slot, m_old):
        bias = score_ref[chunk(c), :]
        m_loc = []
        for hh in range(N_HEADS):
            hs = slice(hh * HEAD_DIM, (hh + 1) * HEAD_DIM)
            s = _dot_nt(k_ref[chunk(c), hs], q_ref[:, hs]) + bias
            s_ref[slot, hh] = s
            m_loc.append(jnp.max(s, axis=0, keepdims=True))
        return jnp.maximum(m_old, jnp.concatenate(m_loc, axis=0))

    def softmax_pv(c, slot, m_acc, m_cur):
        m_safe = jnp.where(m_cur == -jnp.inf, 0.0, m_cur)
        alpha = jnp.exp2(m_acc - m_safe)
        for hh in range(N_HEADS):
            vs = slice(hh * V_ROWS, (hh + 1) * V_ROWS)
            p = jnp.exp2(s_ref[slot, hh] - m_safe[hh:hh + 1, :])
            acc_ref[vs, :] = alpha[hh:hh + 1, :] * acc_ref[vs, :] + _dot(vt_ref[vs, chunk(c)], p.astype(BF16))

    def pair_body(i, carry):
        m_acc, m_cur = carry
        m_1 = logits(2 * i + 1, 1, m_cur)
        softmax_pv(2 * i, 0, m_acc, m_cur)
        m_2 = logits(jnp.minimum(2 * i + 2, n_ch - 1), 0, m_1)
        softmax_pv(2 * i + 1, 1, m_cur, m_1)
        return m_1, m_2

    no_max = jnp.full((N_HEADS, tq), -jnp.inf, F32)
    m_acc, m_cur = lax.fori_loop(0, n_ch // 2, pair_body, (no_max, logits(0, 0, no_max)))

    @pl.when(n_ch % 2 == 1)
    def _():
        softmax_pv(n_ch - 1, 0, m_acc, m_cur)
    for hh in range(N_HEADS):
        r0 = hh * V_ROWS
        out = acc_ref[r0:r0 + HEAD_DIM, :] / acc_ref[r0 + HEAD_DIM:r0 + HEAD_DIM + 1, :]
        o_ref[:, hh * HEAD_DIM:(hh + 1) * HEAD_DIM] = out.T.astype(BF16)


def _attn_prompt(qb, qib, wi, kb, vt, kib, batch, seq, topk, stacked=()):
    n, d = qb.shape
    tq = min(ATTN_Q_TILE, seq)
    kc = min(ATTN_K_CHUNK, seq)
    nq = seq // tq
    n_layers = 2
    n_arrays = len(stacked) // n_layers
    tile_map = lambda b, j: (b * nq + j, 0)
    batch_map = lambda b, j: (b, 0)
    stack_spec = pl.BlockSpec((n_layers, tq, d), lambda b, j: (0, b * nq + j, 0))
    outs = pl.pallas_call(
        functools.partial(_attn_prompt_kernel, topk=topk, n_stacked=len(stacked)),
        grid=(batch, nq),
        in_specs=[pl.BlockSpec((tq, d), tile_map), pl.BlockSpec((tq, d), tile_map),
                  pl.BlockSpec((tq, LANES), tile_map),
                  pl.BlockSpec((seq, d), batch_map, pipeline_mode=pl.Buffered(1)),
                  pl.BlockSpec((None, N_HEADS * V_ROWS, seq), lambda b, j: (b, 0, 0), pipeline_mode=pl.Buffered(1)),
                  pl.BlockSpec((seq, IDX_DIM), batch_map, pipeline_mode=pl.Buffered(1))]
                 + [pl.BlockSpec((tq, d), tile_map)] * len(stacked),
        out_specs=[pl.BlockSpec((tq, d), tile_map)] + [stack_spec] * n_arrays,
        out_shape=[jax.ShapeDtypeStruct((n, d), BF16)] + [jax.ShapeDtypeStruct((n_layers, n, d), F32)] * n_arrays,
        scratch_shapes=[pltpu.VMEM((seq, tq), F32), pltpu.VMEM((LANES, tq), F32),
                        pltpu.VMEM((N_HEADS * V_ROWS, tq), F32), pltpu.VMEM((2, N_HEADS, kc, tq), F32)],
        compiler_params=_compiler_params(("parallel", "arbitrary")),
        name="attn_prompt",
    )(qb, qib, wi, kb, vt, kib, *stacked)
    return outs if stacked else outs[0]


def _sample_scores_kernel(pt_ref, qi_ref, w_ref, kin_ref, *rest, pages_per_step):
    del pt_ref
    page_refs = rest[:pages_per_step]
    score_ref, self_ref = rest[pages_per_step:]
    qi = qi_ref[...]
    w = w_ref[...]

    rows = []
    for r in range(pages_per_step):
        d = _dot_nt(qi, page_refs[r][...].astype(BF16))
        rows.append(jnp.sum(w * jnp.maximum(d, 0.0), axis=0, keepdims=True))
    score_ref[...] = jnp.concatenate(rows, axis=0)

    @pl.when(pl.program_id(1) == 0)
    def _():
        d_self = jnp.sum(qi.astype(F32) * kin_ref[...].astype(BF16).astype(F32), axis=1, keepdims=True)
        self_ref[...] = jnp.sum(w * jnp.maximum(d_self, 0.0), axis=0, keepdims=True)


def _sample_scores(qi8, w8, ki_new, cache_kidx, page_table, layer):
    db, n_pages = page_table.shape
    pps = min(IDX_PAGES_PER_STEP, n_pages)
    steps = n_pages // pps

    def page_spec(r):
        return pl.BlockSpec((None, None, PAGE_SIZE, IDX_DIM), lambda b, g, pt: (layer, pt[b, g * pps + r], 0, 0))

    per_seq = lambda b, g, pt: (b, 0, 0)
    grid_spec = pltpu.PrefetchScalarGridSpec(
        num_scalar_prefetch=1,
        grid=(db, steps),
        in_specs=[pl.BlockSpec((None, IDX_HEADS, IDX_DIM), per_seq), pl.BlockSpec((None, IDX_HEADS, LANES), per_seq),
                  pl.BlockSpec((None, 1, IDX_DIM), per_seq)] + [page_spec(r) for r in range(pps)],
        out_specs=[pl.BlockSpec((None, pps, PAGE_SIZE), lambda b, g, pt: (b, g, 0)),
                   pl.BlockSpec((None, 1, LANES), per_seq)],
    )
    return pl.pallas_call(
        functools.partial(_sample_scores_kernel, pages_per_step=pps),
        grid_spec=grid_spec,
        out_shape=(jax.ShapeDtypeStruct((db, n_pages, PAGE_SIZE), F32), jax.ShapeDtypeStruct((db, 1, LANES), F32)),
        compiler_params=_compiler_params(("parallel", "arbitrary")),
        name="sample_scores",
    )(page_table, qi8, w8, ki_new, *([cache_kidx] * pps))


def _sample_select_kernel(x_ref, xself_ref, sel_ref, selself_ref, *, topk):
    x = x_ref[...]
    x_self = xself_ref[...][:, :, :1]
    db, n_pages, ps = x.shape

    def total(v, op):
        return op(op(v, axis=1, keepdims=True), axis=2, keepdims=True)

    def count_ge(p):
        return total(jnp.where(x >= p, 1.0, 0.0), jnp.sum) + jnp.where(x_self >= p, 1.0, 0.0)

    def max_below(hi):
        return jnp.maximum(total(jnp.where(x < hi, x, -jnp.inf), jnp.max), jnp.where(x_self < hi, x_self, -jnp.inf))

    row_max = jnp.maximum(total(x, jnp.max), x_self)
    row_min = jnp.minimum(total(x, jnp.min), x_self)
    n_valid = jnp.full((db, 1, 1), float(n_pages * ps + 1), F32)
    kk = jnp.full((db, 1, 1), float(topk), F32)
    lo, hi, clo, chi = _topk_bracket(count_ge, max_below, row_min, row_max, n_valid, kk)
    has_ties = _any(clo != kk) > 0

    @pl.when(jnp.logical_not(has_ties))
    def _():
        sel_ref[...] = jnp.where(x >= lo, 1.0, 0.0)
        selself_ref[...] = jnp.broadcast_to(jnp.where(x_self >= lo, 1.0, 0.0), selself_ref.shape)

    @pl.when(has_ties)
    def _():
        need = kk - chi
        cand = jnp.where((x >= lo) & (x < hi), 1.0, 0.0)
        upto = jnp.where(lax.broadcasted_iota(jnp.int32, (ps, ps), 0) <= lax.broadcasted_iota(jnp.int32, (ps, ps), 1),
                         1.0, 0.0).astype(BF16)
        below = jnp.where(lax.broadcasted_iota(jnp.int32, (n_pages, n_pages), 1)
                          < lax.broadcasted_iota(jnp.int32, (n_pages, n_pages), 0), 1.0, 0.0).astype(BF16)
        in_page = _dot(cand.reshape(db * n_pages, ps).astype(BF16), upto).reshape(db, n_pages, ps)
        page_tot = jnp.broadcast_to(jnp.sum(cand, axis=2, keepdims=True), cand.shape).astype(BF16)
        for b in range(db):
            rank = in_page[b] + _dot(below, page_tot[b])
            take = (cand[b] > 0.0) & (rank <= need[b])
            sel_ref[b] = jnp.where((x[b] >= hi[b]) | take, 1.0, 0.0)
        cand_self = (x_self >= lo) & (x_self < hi)
        take_self = cand_self & (total(cand, jnp.sum) + 1.0 <= need)
        selself_ref[...] = jnp.broadcast_to(jnp.where((x_self >= hi) | take_self, 1.0, 0.0), selself_ref.shape)


def _sample_select(scores, score_self, topk):
    args = (scores, score_self)
    return pl.pallas_call(
        functools.partial(_sample_select_kernel, topk=topk),
        grid=(1,),
        in_specs=[_resident(a.shape) for a in args],
        out_specs=[pl.BlockSpec(a.shape, lambda i: (0, 0, 0)) for a in args],
        out_shape=tuple(jax.ShapeDtypeStruct(a.shape, F32) for a in args),
        compiler_params=_compiler_params(("arbitrary",)),
        name="sample_select",
    )(*args)


def _sample_attn_kernel(pt_ref, q_ref, sel_ref, selself_ref, kn_ref, vn_ref, spread_ref, *rest, pages_per_step):
    del pt_ref
    k_refs = rest[:pages_per_step]
    v_refs = rest[pages_per_step:2 * pages_per_step]
    o_ref, m_ref, l_ref, acc_ref = rest[2 * pages_per_step:]
    g = pl.program_id(1)
    q = q_ref[...]
    q_bf = q.astype(BF16)
    cols = PAGE_SIZE * N_HEADS
    own = (lax.broadcasted_iota(jnp.int32, (N_HEADS, cols), 1) % N_HEADS
           == lax.broadcasted_iota(jnp.int32, (N_HEADS, cols), 0))

    @pl.when(g == 0)
    def _():
        m_ref[...] = jnp.full(m_ref.shape, -jnp.inf, F32)
        l_ref[...] = jnp.zeros(l_ref.shape, F32)
        acc_ref[...] = jnp.zeros(acc_ref.shape, F32)

    def accumulate(logits, value_fns):
        m_old = m_ref[...]
        m_new = m_old
        for s in logits:
            m_new = jnp.maximum(m_new, jnp.max(s, axis=1, keepdims=True))
        m_safe = jnp.where(m_new == -jnp.inf, 0.0, m_new)
        alpha = jnp.exp2(m_old - m_safe)
        l = alpha * l_ref[...]
        acc = alpha * acc_ref[...]
        for s, value_fn in zip(logits, value_fns):
            p = jnp.exp2(s - m_safe)
            l = l + jnp.sum(p, axis=1, keepdims=True)
            acc = acc + value_fn(p)
        l_ref[...] = l
        acc_ref[...] = acc
        m_ref[...] = m_new

    sel_cols = _dot(sel_ref[...].astype(BF16), spread_ref[...])
    logits = []
    for r in range(pages_per_step):
        s = _dot_nt(q_bf, k_refs[r][...].astype(BF16))
        logits.append(jnp.where(own & (sel_cols[r:r + 1, :] > 0.0), s, -jnp.inf))
    accumulate(logits, [lambda p, r=r: _dot(p.astype(BF16), v_refs[r][...].astype(BF16))
                        for r in range(pages_per_step)])

    @pl.when(g == pl.num_programs(1) - 1)
    def _():
        kn = kn_ref[...].astype(BF16).astype(F32)
        s_self = jnp.sum(q * kn, axis=1, keepdims=True)
        s_self = jnp.where(selself_ref[:, :1] > 0.0, s_self, -jnp.inf)
        vn = vn_ref[...].astype(BF16).astype(F32)
        accumulate([s_self], [lambda p: p.astype(BF16).astype(F32) * vn])
        o_ref[...] = acc_ref[...] / l_ref[...]


def _sample_attn(q, sel, sel_self, k_new, v_new, cache_k, cache_v, page_table, layer):
    db, n_pages = page_table.shape
    pps = min(KV_PAGES_PER_STEP, n_pages)
    steps = n_pages // pps
    cols = PAGE_SIZE * N_HEADS
    sel4 = sel.reshape(db, steps, pps, PAGE_SIZE)
    spread = (jnp.arange(cols)[None, :] // N_HEADS == jnp.arange(PAGE_SIZE)[:, None]).astype(BF16)

    def page_spec(r):
        return pl.BlockSpec((None, None, cols, HEAD_DIM), lambda b, g, pt: (layer, pt[b, g * pps + r], 0, 0))

    per_seq = lambda b, g, pt: (b, 0, 0)
    heads_spec = pl.BlockSpec((None, N_HEADS, HEAD_DIM), per_seq)
    grid_spec = pltpu.PrefetchScalarGridSpec(
        num_scalar_prefetch=1,
        grid=(db, steps),
        in_specs=[heads_spec,
                  pl.BlockSpec((None, None, pps, PAGE_SIZE), lambda b, g, pt: (b, g, 0, 0)),
                  pl.BlockSpec((None, 1, LANES), per_seq), heads_spec, heads_spec,
                  pl.BlockSpec((PAGE_SIZE, cols), lambda b, g, pt: (0, 0))]
                 + [page_spec(r) for r in range(pps)] * 2,
        out_specs=heads_spec,
        scratch_shapes=[pltpu.VMEM((N_HEADS, 1), F32), pltpu.VMEM((N_HEADS, 1), F32),
                        pltpu.VMEM((N_HEADS, HEAD_DIM), F32)],
    )
    return pl.pallas_call(
        functools.partial(_sample_attn_kernel, pages_per_step=pps),
        grid_spec=grid_spec,
        out_shape=jax.ShapeDtypeStruct((db, N_HEADS, HEAD_DIM), F32),
        compiler_params=_compiler_params(("parallel", "arbitrary")),
        name="sample_attn",
    )(page_table, q, sel4, sel_self, k_new, v_new, spread, *([cache_k] * pps), *([cache_v] * pps))


def _ln_swish(y, g, b):
    yc = y - jnp.mean(y, axis=-1, keepdims=True)
    yn = yc * lax.rsqrt(jnp.mean(yc * yc, axis=-1, keepdims=True) + EPS) * g + b
    return yn * jax.nn.sigmoid(yn)


def _conv_prompt_kernel(cur_ref, halo_ref, w_ref, bdw_ref, lng_ref, lnb_ref, y_ref, sh_ref, conv_ref):
    tt, d = cur_ref.shape
    sh_ref[0, :CONV_HALO, :] = jnp.where(pl.program_id(1) == 0, 0.0, halo_ref[...])
    sh_ref[0, CONV_HALO:, :] = cur_ref[...]
    n_sh = tt + CONV_HALO - SUBLANES
    for r in range(1, SUBLANES):
        sh_ref[r, :n_sh, :] = sh_ref[0, r:r + n_sh, :]
    first = CONV_HALO - CONV_CTX

    def block(rb, carry):
        r0 = pl.multiple_of(rb * CONV_ROW_BLOCK, CONV_ROW_BLOCK)
        for lt in range(d // LANES):
            ls = slice(lt * LANES, (lt + 1) * LANES)
            acc = jnp.zeros((CONV_ROW_BLOCK, LANES), F32)
            for r in range(SUBLANES):
                taps = [j for j in range(CONV_WIDTH) if (j + first) % SUBLANES == r]
                a_max = max((j + first) // SUBLANES for j in taps)
                rows = sh_ref[r, pl.ds(r0, a_max * SUBLANES + CONV_ROW_BLOCK), ls]
                for j in taps:
                    a = (j + first) // SUBLANES
                    acc = acc + w_ref[j:j + 1, ls] * rows[a * SUBLANES:a * SUBLANES + CONV_ROW_BLOCK]
            conv_ref[pl.ds(r0, CONV_ROW_BLOCK), ls] = acc
        return carry

    lax.fori_loop(0, tt // CONV_ROW_BLOCK, block, 0)
    y = conv_ref[...] + bdw_ref[...]
    y_ref[...] = _ln_swish(y, lng_ref[...], lnb_ref[...]).astype(BF16)


def _conv_prompt(glu, w_dw, b_dw, ln_g, ln_b, batch, seq):
    n, d = glu.shape
    tt = min(CONV_T_TILE, seq)
    nt = seq // tt
    halo_per_tile = tt // CONV_HALO
    halo_per_seq = seq // CONV_HALO
    tile_map = lambda b, j: (b * nt + j, 0)
    halo_map = lambda b, j: (jnp.maximum(b * halo_per_seq + j * halo_per_tile - 1, 0), 0)
    consts = (w_dw, b_dw, ln_g, ln_b)
    return pl.pallas_call(
        _conv_prompt_kernel,
        grid=(batch, nt),
        in_specs=[pl.BlockSpec((tt, d), tile_map), pl.BlockSpec((CONV_HALO, d), halo_map)]
                 + [_resident(c.shape) for c in consts],
        out_specs=pl.BlockSpec((tt, d), tile_map),
        out_shape=jax.ShapeDtypeStruct((n, d), BF16),
        scratch_shapes=[pltpu.VMEM((SUBLANES, tt + CONV_HALO, d), F32), pltpu.VMEM((tt, d), F32)],
        compiler_params=_compiler_params(("parallel", "arbitrary")),
        name="conv_prompt",
    )(glu, glu, *consts)


def _conv_sample_kernel(state_ref, u_ref, w_ref, bdw_ref, lng_ref, lnb_ref, y_ref, conv_ref):
    db = u_ref.shape[0]
    w_ctx = w_ref[:CONV_CTX, :]
    for b in range(db):
        conv_ref[b:b + 1, :] = jnp.sum(state_ref[b] * w_ctx, axis=0, keepdims=True)
    y = conv_ref[...] + u_ref[...] * w_ref[CONV_CTX:CONV_CTX + 1, :] + bdw_ref[...]
    y_ref[...] = _ln_swish(y, lng_ref[...], lnb_ref[...]).astype(BF16)


def _conv_sample(state_all, layer, u, w_dw, b_dw, ln_g, ln_b):
    db, d = u.shape
    args = (state_all, u, w_dw, b_dw, ln_g, ln_b)
    state_spec = pl.BlockSpec((None,) + state_all.shape[1:], lambda i: (layer, 0, 0, 0),
                              pipeline_mode=pl.Buffered(1))
    return pl.pallas_call(
        _conv_sample_kernel,
        grid=(1,),
        in_specs=[state_spec] + [_resident(a.shape) for a in args[1:]],
        out_specs=pl.BlockSpec((db, d), lambda i: (0, 0)),
        out_shape=jax.ShapeDtypeStruct((db, d), BF16),
        scratch_shapes=[pltpu.VMEM((db, d), F32)],
        compiler_params=_compiler_params(("arbitrary",)),
        name="conv_sample",
    )(*args)


def kernel(x_prompt, x_sample, p_prompt, p_sample, cache_k, cache_v, cache_kidx, state_conv, page_table, ffn1_norm, ffn1_w_gate, ffn1_w_up, ffn1_w_down, mix_norm, attn_w_in, attn_q_gain, attn_k_gain, attn_kidx_gain, attn_w_o, conv_w_in, conv_b_in, conv_w_dw, conv_b_dw, conv_ln_g, conv_ln_b, conv_w_out, conv_b_out, ffn2_norm, ffn2_w_gate, ffn2_w_up, ffn2_w_down, ple_norm, ple_w_gate, ple_w_proj):
    batch, seq, d = x_prompt.shape
    db, t_new, _ = x_sample.shape
    depth = ffn1_norm.shape[0]
    n_mixers = 2
    n_pages = page_table.shape[1]
    past = n_pages * PAGE_SIZE
    attn_dim = N_HEADS * HEAD_DIM
    assert t_new == 1 and d == attn_dim
    topk_p = min(TOPK_MAX, seq // 4)
    topk_s = min(TOPK_MAX, (past + t_new) // 4)
    n_p = batch * seq
    tile_p = min(ROW_TILE, seq)
    tile_s = db

    row = lambda v: v.reshape(1, -1)
    bf = lambda w: w.astype(BF16)
    hp = x_prompt.reshape(n_p, d)
    hs = x_sample.reshape(db, d)
    cache_k4 = cache_k.reshape(cache_k.shape[0], cache_k.shape[1], PAGE_SIZE * N_HEADS, HEAD_DIM)
    cache_v4 = cache_v.reshape(cache_v.shape[0], cache_v.shape[1], PAGE_SIZE * N_HEADS, HEAD_DIM)
    zero_bias = jnp.zeros((1, d), F32)
    n_attn = attn_w_in.shape[0]
    pp_all = p_prompt.reshape(depth, n_p, -1)
    ps_all = p_sample.reshape(depth, db, -1)

    k_p, v_p, ki_p, conv_p = [], [], [], []
    k_s, v_s, ki_s, conv_s = [], [], [], []
    for i in range(depth):
        ffn1 = (row(ffn1_norm[i]), bf(ffn1_w_gate[i]), bf(ffn1_w_up[i]), bf(ffn1_w_down[i]))
        ffn2 = (row(ffn2_norm[i]), bf(ffn2_w_gate[i]), bf(ffn2_w_up[i]), bf(ffn2_w_down[i]))
        ple = (row(ple_norm[i]), bf(ple_w_gate[i]), bf(ple_w_proj[i]))
        g_mix = row(mix_norm[i])
        if i % n_mixers == 0:
            a = i // n_mixers
            w_in = bf(attn_w_in[a])
            o_wi = 3 * attn_dim + IDX_HEADS * IDX_DIM + IDX_DIM
            w_wi = jnp.pad(w_in.T[o_wi:], ((0, LANES - IDX_HEADS), (0, 0))).T
            proj = (w_in, w_wi, row(attn_q_gain[a]), row(attn_k_gain[a]), row(attn_kidx_gain[a]))
            w_out, b_out = bf(attn_w_o[a]), zero_bias

            hp, qb, kf, vf, kif, kb, vt, qib, kib, wi = _stage_a_attn(hp, ffn1, g_mix, proj, tile_p, batch)
            k_p.append(kf)
            v_p.append(vf)
            ki_p.append(kif.reshape(batch, seq, IDX_DIM))
            if n_attn == 2 and a == n_attn - 1:
                mp, k_stack, v_stack = _attn_prompt(qb, qib, wi, kb, vt, kib, batch, seq, topk_p,
                                                    stacked=(k_p[0], v_p[0], k_p[1], v_p[1]))
            else:
                mp = _attn_prompt(qb, qib, wi, kb, vt, kib, batch, seq, topk_p)

            hs, qb, kf, vf, kif, kb, vt, qib, kib, wi = _stage_a_attn(hs, ffn1, g_mix, proj, tile_s, 1)
            qi8 = qib.reshape(db, IDX_HEADS, IDX_DIM)
            w8 = jnp.broadcast_to(wi[:, :IDX_HEADS, None], (db, IDX_HEADS, LANES))
            scores, score_self = _sample_scores(qi8, w8, kif.reshape(db, 1, IDX_DIM), cache_kidx, page_table, a)
            sel, sel_self = _sample_select(scores, score_self, topk_s)
            per_head = lambda v: v.astype(F32).reshape(db, N_HEADS, HEAD_DIM)
            ms = _sample_attn(per_head(qb), sel, sel_self, per_head(kf), per_head(vf), cache_k4, cache_v4,
                              page_table, a)
            ms = ms.reshape(db, d).astype(BF16)
            k_s.append(kf.reshape(db, t_new, N_HEADS, HEAD_DIM))
            v_s.append(vf.reshape(db, t_new, N_HEADS, HEAD_DIM))
            ki_s.append(kif.reshape(db, t_new, IDX_DIM))
        else:
            c = i // n_mixers
            w_dw = jnp.pad(conv_w_dw[c], ((0, CONV_HALO - CONV_WIDTH), (0, 0)))
            conv = (w_dw, row(conv_b_dw[c]), row(conv_ln_g[c]), row(conv_ln_b[c]))
            w_out, b_out = bf(conv_w_out[c]), row(conv_b_out[c])

            hp, glu = _stage_a_conv(hp, ffn1, g_mix, bf(conv_w_in[c]), row(conv_b_in[c]), tile_p)
            mp = _conv_prompt(glu, *conv, batch, seq)
            conv_p.append(glu.reshape(batch, seq, d)[:, seq - CONV_CTX:])

            hs, glu = _stage_a_conv(hs, ffn1, g_mix, bf(conv_w_in[c]), row(conv_b_in[c]), tile_s)
            ms = _conv_sample(state_conv, c, glu, *conv)
            conv_s.append(jnp.concatenate([state_conv[c], glu[:, None, :]], axis=1)[:, -CONV_CTX:])

        hp = _stage_c(hp, mp, pp_all, i, w_out, b_out, ffn2, ple, tile_p)
        hs = _stage_c(hs, ms, ps_all, i, w_out, b_out, ffn2, ple, tile_s)

    if n_attn != 2:
        k_stack, v_stack = jnp.stack(k_p), jnp.stack(v_p)
    kv_shape = (n_attn, batch, seq, N_HEADS, HEAD_DIM)
    return (hp.reshape(batch, seq, d), hs.reshape(db, t_new, d), k_stack.reshape(kv_shape), v_stack.reshape(kv_shape),
            jnp.stack(ki_p),
            jnp.stack(conv_p), jnp.stack(k_s), jnp.stack(v_s), jnp.stack(ki_s), jnp.stack(conv_s))
```
